```python
import jax, jax.numpy as jnp
from jax import lax
import numpy as np

D_MODEL = 4096
BATCH = 1
SEQ = 8192
DEPTH = 2

GRID_W = 64
CTX_LEN = 256
HEAD_DIM = 128
ATTN_HEADS = D_MODEL // (2 * HEAD_DIM)
ATTN_KV_HEADS = ATTN_HEADS // 4
GQA_GROUP = ATTN_HEADS // ATTN_KV_HEADS
ATTN_WIDTH = ATTN_HEADS * HEAD_DIM
KV_WIDTH = ATTN_KV_HEADS * HEAD_DIM
CONV_WIDTH = D_MODEL - ATTN_WIDTH
CONV_KERNEL = 31
GMLP_WIDTH = D_MODEL
GMLP_HEADS = 16
CHUNK = 128
Q_BLOCK = 128
ROPE_THETA = 10000.0
PEER_HEADS = 8
PEER_KEYS = 128
PEER_EXPERTS = PEER_KEYS * PEER_KEYS
PEER_KEY_DIM = 256
PEER_TOPK = 16
PEER_BLOCK = 128
LN_EPS = 1e-6
DEEPNORM_ALPHA = (2 * DEPTH) ** 0.25
DEEPNORM_BETA = (8 * DEPTH) ** -0.25
N_MOD = 6

kernel_name = "hybrid_attn_conv_gmlp_peer_dit"


def layer_norm(x, g, b):
    xf = x.astype(jnp.float32)
    mu = jnp.mean(xf, axis=-1, keepdims=True)
    var = jnp.mean(jnp.square(xf - mu), axis=-1, keepdims=True)
    return ((xf - mu) * lax.rsqrt(var + LN_EPS) * g + b).astype(x.dtype)


def rms_norm(x, g):
    xf = x.astype(jnp.float32)
    return (xf * lax.rsqrt(jnp.mean(jnp.square(xf), axis=-1, keepdims=True) + LN_EPS) * g).astype(x.dtype)


def modulate(x, shift, scale):
    return x * (1 + scale[:, None, :]) + shift[:, None, :]


def rope_2d(x, rows, cols):
    n_freq = HEAD_DIM // 4
    inv = ROPE_THETA ** (-jnp.arange(n_freq, dtype=jnp.float32) / n_freq)

    def rot(xa, pos):
        ang = pos.astype(jnp.float32)[:, None] * inv
        cos = jnp.cos(ang)[None, :, None, :]
        sin = jnp.sin(ang)[None, :, None, :]
        x1, x2 = jnp.split(xa.astype(jnp.float32), 2, axis=-1)
        return jnp.concatenate([x1 * cos - x2 * sin, x1 * sin + x2 * cos], axis=-1)

    x_row, x_col = jnp.split(x, 2, axis=-1)
    return jnp.concatenate([rot(x_row, rows), rot(x_col, cols)], axis=-1).astype(x.dtype)


def gqa_attention(q, k, v, kc, vc, q_gain, k_gain, rows, cols):
    B, S = q.shape[:2]
    q = rope_2d(rms_norm(q, q_gain), rows, cols)
    k = rope_2d(rms_norm(k, k_gain), rows, cols)
    k_all = jnp.concatenate([kc, k], axis=1)
    v_all = jnp.concatenate([vc, v], axis=1)
    n_blk = S // Q_BLOCK
    qb = q.reshape(B, n_blk, Q_BLOCK, ATTN_KV_HEADS, GQA_GROUP, HEAD_DIM).transpose(1, 0, 2, 3, 4, 5)
    scale = HEAD_DIM ** -0.5

    def one_block(q_blk):
        s = jnp.einsum('bqkgd,bskd->bkgqs', q_blk, k_all, preferred_element_type=jnp.float32) * scale
        p = jax.nn.softmax(s, axis=-1).astype(v_all.dtype)
        return jnp.einsum('bkgqs,bskd->bqkgd', p, v_all)

    o = lax.map(one_block, qb)
    return o.transpose(1, 0, 2, 3, 4, 5).reshape(B, S, ATTN_WIDTH)


def conformer_conv(a, dw_w, dw_b, ln_g, ln_b):
    val, gate = jnp.split(a, 2, axis=-1)
    h = val * jax.nn.sigmoid(gate)
    h = lax.conv_general_dilated(
        h, dw_w[:, None, :], window_strides=(1,),
        padding=[(CONV_KERNEL // 2, CONV_KERNEL // 2)],
        dimension_numbers=('NWC', 'WIO', 'NWC'),
        feature_group_count=CONV_WIDTH) + dw_b
    return jax.nn.silu(layer_norm(h, ln_g, ln_b))


def attn_conv_mixer(h, hc, rows, cols, w_in, q_gain, k_gain, dw_w, dw_b, conv_ln_g, conv_ln_b, w_out):
    B, S, _ = h.shape
    n_ctx = hc.shape[1]
    z = h @ w_in
    q, k, v, a = jnp.split(z, [ATTN_WIDTH, ATTN_WIDTH + KV_WIDTH, ATTN_WIDTH + 2 * KV_WIDTH], axis=-1)
    kvc = hc @ w_in[:, ATTN_WIDTH:ATTN_WIDTH + 2 * KV_WIDTH]
    kc, vc = jnp.split(kvc.reshape(B, n_ctx, 2 * ATTN_KV_HEADS, HEAD_DIM), 2, axis=2)
    kc = rms_norm(kc, k_gain)
    o_attn = gqa_attention(
        q.reshape(B, S, ATTN_HEADS, HEAD_DIM),
        k.reshape(B, S, ATTN_KV_HEADS, HEAD_DIM),
        v.reshape(B, S, ATTN_KV_HEADS, HEAD_DIM),
        kc, vc, q_gain, k_gain, rows, cols)
    o_conv = conformer_conv(a, dw_w, dw_b, conv_ln_g, conv_ln_b)
    return jnp.concatenate([o_attn, o_conv], axis=-1) @ w_out


def chunk_gmlp_mixer(h, w_in, sgu_ln_g, sgu_ln_b, sgu_w, sgu_b, w_out):
    B, S, _ = h.shape
    u, v = jnp.split(jax.nn.gelu(h @ w_in), 2, axis=-1)
    v = layer_norm(v, sgu_ln_g, sgu_ln_b)
    v = v.reshape(B, S // CHUNK, CHUNK, GMLP_HEADS, GMLP_WIDTH // GMLP_HEADS)
    mixed = jnp.einsum('hpq,bnqhc->bnphc', sgu_w, v) + sgu_b.T[None, None, :, :, None]
    return (u * mixed.reshape(B, S, GMLP_WIDTH)) @ w_out


def peer_ffn(h, w_q, sub_k1, sub_k2, u_tab, v_tab):
    B, S, D = h.shape
    T = B * S
    t = h.reshape(T, D)
    q = (t @ w_q).reshape(T, PEER_HEADS, PEER_KEY_DIM)
    q1, q2 = jnp.split(q, 2, axis=-1)
    s1 = jnp.einsum('thd,hkd->thk', q1, sub_k1, preferred_element_type=jnp.float32)
    s2 = jnp.einsum('thd,hkd->thk', q2, sub_k2, preferred_element_type=jnp.float32)
    v1, i1 = lax.top_k(s1, PEER_TOPK)
    v2, i2 = lax.top_k(s2, PEER_TOPK)
    cand = (v1[..., :, None] + v2[..., None, :]).reshape(T, PEER_HEADS, PEER_TOPK * PEER_TOPK)
    score, ci = lax.top_k(cand, PEER_TOPK)
    expert = (jnp.take_along_axis(i1, ci // PEER_TOPK, axis=-1) * PEER_KEYS
              + jnp.take_along_axis(i2, ci % PEER_TOPK, axis=-1))
    gate = jax.nn.softmax(score, axis=-1)
    n_blk = T // PEER_BLOCK

    def one_block(args):
        tb, eb, gb = args
        act = jax.nn.gelu(jnp.einsum('thkd,td->thk', u_tab[eb], tb, preferred_element_type=jnp.float32))
        w = (gb * act).astype(tb.dtype)
        return jnp.einsum('thk,thkd->td', w, v_tab[eb])

    out = lax.map(one_block, (t.reshape(n_blk, PEER_BLOCK, D),
                              expert.reshape(n_blk, PEER_BLOCK, PEER_HEADS, PEER_TOPK),
                              gate.reshape(n_blk, PEER_BLOCK, PEER_HEADS, PEER_TOPK)))
    return out.reshape(B, S, D)


def setup_inputs(seed: int = 0) -> dict:
    key = jax.random.key(seed)
    keys = iter(jax.random.split(key, 128))
    d = D_MODEL

    def nrm(shape, scale):
        return jax.random.normal(next(keys), shape, jnp.float32) * scale

    def gain(shape):
        return 1.0 + nrm(shape, 0.01)

    def bias(shape):
        return nrm(shape, 0.01)

    inp = {
        'x': nrm((BATCH, SEQ, d), 1.0),
        'c': nrm((BATCH, d), 1.0),
        'ctx': nrm((BATCH, CTX_LEN, d), 1.0),
        'c_ctx': nrm((d,), 1.0),
    }
    for i in range(DEPTH):
        p = 'l%d_' % i
        inp[p + 'w_mod'] = nrm((d, N_MOD * d), 0.5 * d ** -0.5)
        inp[p + 'b_mod'] = bias((N_MOD * d,))
        if i % 2 == 0:
            inp[p + 'w_in'] = nrm((d, ATTN_WIDTH + 2 * KV_WIDTH + 2 * CONV_WIDTH), d ** -0.5)
            inp[p + 'q_gain'] = gain((HEAD_DIM,))
            inp[p + 'k_gain'] = gain((HEAD_DIM,))
            inp[p + 'dw_w'] = nrm((CONV_KERNEL, CONV_WIDTH), CONV_KERNEL ** -0.5)
            inp[p + 'dw_b'] = bias((CONV_WIDTH,))
            inp[p + 'conv_ln_g'] = gain((CONV_WIDTH,))
            inp[p + 'conv_ln_b'] = bias((CONV_WIDTH,))
        else:
            inp[p + 'w_in'] = nrm((d, 2 * GMLP_WIDTH), d ** -0.5)
            inp[p + 'sgu_ln_g'] = gain((GMLP_WIDTH,))
            inp[p + 'sgu_ln_b'] = bias((GMLP_WIDTH,))
            inp[p + 'sgu_w'] = nrm((GMLP_HEADS, CHUNK, CHUNK), CHUNK ** -0.5)
            inp[p + 'sgu_b'] = gain((GMLP_HEADS, CHUNK))
        inp[p + 'w_out'] = nrm((d, d), DEEPNORM_BETA * d ** -0.5)
        inp[p + 'mix_ln_g'] = gain((d,))
        inp[p + 'mix_ln_b'] = bias((d,))
        inp[p + 'peer_wq'] = nrm((d, PEER_HEADS * PEER_KEY_DIM), d ** -0.5)
        inp[p + 'peer_k1'] = nrm((PEER_HEADS, PEER_KEYS, PEER_KEY_DIM // 2), (PEER_KEY_DIM // 2) ** -0.5)
        inp[p + 'peer_k2'] = nrm((PEER_HEADS, PEER_KEYS, PEER_KEY_DIM // 2), (PEER_KEY_DIM // 2) ** -0.5)
        inp[p + 'peer_u'] = nrm((PEER_EXPERTS, d), d ** -0.5)
        inp[p + 'peer_v'] = nrm((PEER_EXPERTS, d), DEEPNORM_BETA)
        inp[p + 'ffn_ln_g'] = gain((d,))
        inp[p + 'ffn_ln_b'] = bias((d,))
    return inp


def reference(x, c, ctx, c_ctx,
              l0_w_mod, l0_b_mod, l0_w_in, l0_q_gain, l0_k_gain, l0_dw_w, l0_dw_b, l0_conv_ln_g, l0_conv_ln_b,
              l0_w_out, l0_mix_ln_g, l0_mix_ln_b, l0_peer_wq, l0_peer_k1, l0_peer_k2, l0_peer_u, l0_peer_v,
              l0_ffn_ln_g, l0_ffn_ln_b,
              l1_w_mod, l1_b_mod, l1_w_in, l1_sgu_ln_g, l1_sgu_ln_b, l1_sgu_w, l1_sgu_b,
              l1_w_out, l1_mix_ln_g, l1_mix_ln_b, l1_peer_wq, l1_peer_k1, l1_peer_k2, l1_peer_u, l1_peer_v,
              l1_ffn_ln_g, l1_ffn_ln_b):
    B, S, D = x.shape
    n_rows = S // GRID_W
    rows = jnp.repeat(jnp.arange(n_rows, dtype=jnp.int32), GRID_W)
    cols = jnp.tile(jnp.arange(GRID_W, dtype=jnp.int32), n_rows)
    layers = (
        (l0_w_mod, l0_b_mod,
         (l0_w_in, l0_q_gain, l0_k_gain, l0_dw_w, l0_dw_b, l0_conv_ln_g, l0_conv_ln_b, l0_w_out),
         l0_mix_ln_g, l0_mix_ln_b,
         (l0_peer_wq, l0_peer_k1, l0_peer_k2, l0_peer_u, l0_peer_v),
         l0_ffn_ln_g, l0_ffn_ln_b),
        (l1_w_mod, l1_b_mod,
         (l1_w_in, l1_sgu_ln_g, l1_sgu_ln_b, l1_sgu_w, l1_sgu_b, l1_w_out),
         l1_mix_ln_g, l1_mix_ln_b,
         (l1_peer_wq, l1_peer_k1, l1_peer_k2, l1_peer_u, l1_peer_v),
         l1_ffn_ln_g, l1_ffn_ln_b),
    )
    for i in range(DEPTH):
        w_mod, b_mod, mix_p, mix_g, mix_b, peer_p, ffn_g, ffn_b = layers[i]
        shift_m, scale_m, gate_m, shift_f, scale_f, gate_f = jnp.split(
            jax.nn.silu(c) @ w_mod + b_mod, N_MOD, axis=-1)
        h = modulate(x, shift_m, scale_m)
        if i % 2 == 0:
            ctx_mod = jax.nn.silu(c_ctx)[None, :] @ w_mod[:, :2 * D] + b_mod[:2 * D]
            ctx_shift, ctx_scale = jnp.split(ctx_mod, 2, axis=-1)
            hc = modulate(ctx, ctx_shift, ctx_scale)
            y = attn_conv_mixer(h, hc, rows, cols, *mix_p)
        else:
            y = chunk_gmlp_mixer(h, *mix_p)
        x = layer_norm(DEEPNORM_ALPHA * x + gate_m[:, None, :] * y, mix_g, mix_b)
        h = modulate(x, shift_f, scale_f)
        x = layer_norm(DEEPNORM_ALPHA * x + gate_f[:, None, :] * peer_ffn(h, *peer_p), ffn_g, ffn_b)
    return x
```

```python
import functools

import jax
import jax.numpy as jnp
from jax import lax
from jax.experimental import pallas as pl
from jax.experimental.pallas import tpu as pltpu

F32 = jnp.float32
BF16 = jnp.bfloat16

GRID_W = 64
HEAD_DIM = 128
KV_HEADS = 4
GQA_GROUP = 4
ROPE_THETA = 10000.0
PEER_TOPK = 16
LN_EPS = 1e-6
DEPTH = 2
DEEPNORM_ALPHA = (2 * DEPTH) ** 0.25
N_MOD = 6

LANES = 128
SUBLANES = 8
VMEM_LIMIT = 56 * 1024 * 1024

_NT = (((1,), (1,)), ((), ()))
_TN = (((0,), (0,)), ((), ()))


def _params(sem):
    return pltpu.CompilerParams(dimension_semantics=sem, vmem_limit_bytes=VMEM_LIMIT)


def _ln_rows(x, g, b):
    mu = jnp.mean(x, axis=-1, keepdims=True)
    xc = x - mu
    var = jnp.mean(xc * xc, axis=-1, keepdims=True)
    return xc * lax.rsqrt(var + LN_EPS) * g + b


def _modvec_kernel(c_ref, w_ref, b_ref, o_ref, sb_ref):
    n_vec = sb_ref.shape[0]

    @pl.when(pl.program_id(0) == 0)
    def _():
        cv = c_ref[...]
        s = cv * jax.nn.sigmoid(cv)
        for r in range(n_vec):
            sb_ref[r] = jnp.broadcast_to(s[:, r:r + 1], sb_ref.shape[1:])

    tn = w_ref.shape[1]
    for j in range(tn // LANES):
        cols = slice(j * LANES, (j + 1) * LANES)
        w = w_ref[:, cols]
        for r in range(n_vec):
            o_ref[r:r + 1, cols] = jnp.sum(w * sb_ref[r], axis=0, keepdims=True) + b_ref[:, cols]


def _modvec(c_cols, w_mod, b_mod, tn=512):
    d, n_vec = c_cols.shape
    n = w_mod.shape[1]
    return pl.pallas_call(
        _modvec_kernel,
        grid=(n // tn,),
        in_specs=[pl.BlockSpec((d, n_vec), lambda j: (0, 0)),
                  pl.BlockSpec((d, tn), lambda j: (0, j)),
                  pl.BlockSpec((1, tn), lambda j: (0, j))],
        out_specs=pl.BlockSpec((n_vec, tn), lambda j: (0, j)),
        out_shape=jax.ShapeDtypeStruct((n_vec, n), F32),
        scratch_shapes=[pltpu.VMEM((n_vec, d, LANES), F32)],
        compiler_params=_params(("arbitrary",)),
        name="modvec",
    )(c_cols, w_mod, b_mod.reshape(1, n))


def _modulate_kernel(x_ref, shift_ref, scale_ref, o_ref):
    o_ref[...] = (x_ref[...] * (1.0 + scale_ref[...]) + shift_ref[...]).astype(o_ref.dtype)


def _modulate(x, shift, scale, tm=256):
    t, d = x.shape
    tm = min(tm, t)
    vec = pl.BlockSpec((1, d), lambda i: (0, 0))
    return pl.pallas_call(
        _modulate_kernel,
        grid=(t // tm,),
        in_specs=[pl.BlockSpec((tm, d), lambda i: (i, 0)), vec, vec],
        out_specs=pl.BlockSpec((tm, d), lambda i: (i, 0)),
        out_shape=jax.ShapeDtypeStruct((t, d), BF16),
        compiler_params=_params(("parallel",)),
        name="modulate",
    )(x, shift, scale)


def _cast_kernel(x_ref, o_ref):
    o_ref[...] = x_ref[...].astype(o_ref.dtype)


def _cast_bf16(x, tm=512):
    r, c = x.shape
    return pl.pallas_call(
        _cast_kernel,
        grid=(r // tm,),
        in_specs=[pl.BlockSpec((tm, c), lambda i: (i, 0))],
        out_specs=pl.BlockSpec((tm, c), lambda i: (i, 0)),
        out_shape=jax.ShapeDtypeStruct((r, c), BF16),
        compiler_params=_params(("parallel",)),
        name="cast_bf16",
    )(x)


def _matmul_kernel(*refs, n_a, act):
    a_refs, w_ref, o_ref = refs[:n_a], refs[n_a], refs[n_a + 1]
    w = w_ref[...].astype(BF16)
    acc = None
    k0 = 0
    for a_ref in a_refs:
        k1 = k0 + a_ref.shape[1]
        part = jnp.dot(a_ref[...], w[k0:k1], preferred_element_type=F32)
        acc = part if acc is None else acc + part
        k0 = k1
    if act == "gelu":
        acc = jax.nn.gelu(acc)
    o_ref[...] = acc.astype(o_ref.dtype)


def _matmul(a_list, w, out_dtype, *, n_out=None, col_off=0, act=None, bm=1024, bn=512, name="matmul"):
    m = a_list[0].shape[0]
    k = w.shape[0]
    n_out = w.shape[1] if n_out is None else n_out
    bm = min(bm, m)
    assert sum(a.shape[1] for a in a_list) == k
    assert m % bm == 0 and n_out % bn == 0 and col_off % bn == 0
    off_blocks = col_off // bn
    in_specs = [pl.BlockSpec((bm, a.shape[1]), lambda i, j: (i, 0)) for a in a_list]
    in_specs.append(pl.BlockSpec((k, bn), lambda i, j: (0, j + off_blocks)))
    return pl.pallas_call(
        functools.partial(_matmul_kernel, n_a=len(a_list), act=act),
        grid=(m // bm, n_out // bn),
        in_specs=in_specs,
        out_specs=pl.BlockSpec((bm, bn), lambda i, j: (i, j)),
        out_shape=jax.ShapeDtypeStruct((m, n_out), out_dtype),
        compiler_params=_params(("parallel", "parallel")),
        name=name,
    )(*a_list, w)


def _resln_kernel(x_ref, y_ref, gate_ref, g_ref, b_ref, shift_ref, scale_ref, xo_ref, ho_ref):
    r = DEEPNORM_ALPHA * x_ref[...] + gate_ref[...] * y_ref[...]
    xn = _ln_rows(r, g_ref[...], b_ref[...])
    xo_ref[...] = xn
    ho_ref[...] = (xn * (1.0 + scale_ref[...]) + shift_ref[...]).astype(ho_ref.dtype)


def _resln(x, y, gate, g, b, shift, scale, tm=256):
    t, d = x.shape
    row = pl.BlockSpec((tm, d), lambda i: (i, 0))
    vec = pl.BlockSpec((1, d), lambda i: (0, 0))
    return pl.pallas_call(
        _resln_kernel,
        grid=(t // tm,),
        in_specs=[row, row, vec, vec, vec, vec, vec],
        out_specs=[row, row],
        out_shape=[jax.ShapeDtypeStruct((t, d), F32), jax.ShapeDtypeStruct((t, d), BF16)],
        compiler_params=_params(("parallel",)),
        name="resln",
    )(x, y, gate, g.reshape(1, d), b.reshape(1, d), shift, scale)


def _qk_prep_kernel(z_ref, gain_ref, cos_ref, sin_ref, o_ref, *, n_q_blocks, q_scale):
    x = z_ref[...]
    x = x * lax.rsqrt(jnp.mean(x * x, axis=-1, keepdims=True) + LN_EPS) * gain_ref[0]
    quarter = HEAD_DIM // 4
    lane = lax.broadcasted_iota(jnp.int32, x.shape, 1)
    partner = jnp.where((lane & quarter) == 0,
                        pltpu.roll(x, HEAD_DIM - quarter, axis=1),
                        pltpu.roll(x, quarter, axis=1))
    y = x * cos_ref[...] + partner * sin_ref[...]
    y = y * jnp.where(pl.program_id(1) < n_q_blocks, q_scale, 1.0)
    o_ref[...] = y.astype(o_ref.dtype)


def _qk_prep(z, gains, cos_t, sin_t, n_blocks, n_q_blocks, q_scale, tm=512):
    t = z.shape[0]
    tm = min(tm, t)
    return pl.pallas_call(
        functools.partial(_qk_prep_kernel, n_q_blocks=n_q_blocks, q_scale=q_scale),
        grid=(t // tm, n_blocks),
        in_specs=[pl.BlockSpec((tm, HEAD_DIM), lambda i, j: (i, j)),
                  pl.BlockSpec((1, 1, HEAD_DIM), lambda i, j: (j, 0, 0)),
                  pl.BlockSpec((tm, HEAD_DIM), lambda i, j: (i, 0)),
                  pl.BlockSpec((tm, HEAD_DIM), lambda i, j: (i, 0))],
        out_specs=pl.BlockSpec((tm, HEAD_DIM), lambda i, j: (i, j)),
        out_shape=jax.ShapeDtypeStruct((t, n_blocks * HEAD_DIM), BF16),
        compiler_params=_params(("parallel", "arbitrary")),
        name="qk_prep",
    )(z, gains, cos_t, sin_t)


def _flash_kernel(q_ref, k_ref, v_ref, o_ref, m_ref, l_ref, acc_ref):
    kt = pl.program_id(2)

    @pl.when(kt == 0)
    def _():
        m_ref[...] = jnp.full(m_ref.shape, -jnp.inf, F32)
        l_ref[...] = jnp.zeros(l_ref.shape, F32)
        acc_ref[...] = jnp.zeros(acc_ref.shape, F32)

    k = k_ref[...]
    v = v_ref[...]
    for g in range(GQA_GROUP):
        cols = slice(g * HEAD_DIM, (g + 1) * HEAD_DIM)
        s = lax.dot_general(q_ref[:, cols], k, _NT, preferred_element_type=F32)
        m_prev = m_ref[g]
        m_new = jnp.maximum(m_prev, jnp.max(s, axis=1, keepdims=True))
        alpha = jnp.exp(m_prev - m_new)
        p = jnp.exp(s - m_new)
        l_ref[g] = alpha * l_ref[g] + jnp.sum(p, axis=1, keepdims=True)
        acc_ref[:, cols] = alpha * acc_ref[:, cols] + jnp.dot(
            p.astype(v.dtype), v, preferred_element_type=F32)
        m_ref[g] = m_new

    @pl.when(kt == pl.num_programs(2) - 1)
    def _():
        for g in range(GQA_GROUP):
            cols = slice(g * HEAD_DIM, (g + 1) * HEAD_DIM)
            o_ref[:, cols] = (acc_ref[:, cols] / l_ref[g]).astype(o_ref.dtype)


def _flash(q, k_all, v_all, tq=1024, tk=768):
    s_q = q.shape[0]
    s_k = k_all.shape[0]
    tq = min(tq, s_q)
    if s_k % tk:
        tk = 256
    gw = GQA_GROUP * HEAD_DIM
    return pl.pallas_call(
        _flash_kernel,
        grid=(KV_HEADS, s_q // tq, s_k // tk),
        in_specs=[pl.BlockSpec((tq, gw), lambda h, i, j: (i, h)),
                  pl.BlockSpec((tk, HEAD_DIM), lambda h, i, j: (j, h)),
                  pl.BlockSpec((tk, HEAD_DIM), lambda h, i, j: (j, h))],
        out_specs=pl.BlockSpec((tq, gw), lambda h, i, j: (i, h)),
        out_shape=jax.ShapeDtypeStruct((s_q, KV_HEADS * gw), BF16),
        scratch_shapes=[pltpu.VMEM((GQA_GROUP, tq, 1), F32),
                        pltpu.VMEM((GQA_GROUP, tq, 1), F32),
                        pltpu.VMEM((tq, gw), F32)],
        compiler_params=_params(("parallel", "parallel", "arbitrary")),
        name="flash",
    )(q, k_all, v_all)


_CONV_HALO = 16
_CONV_ROWS = 64


def _conv_kernel(*refs, n_cb, cb_w, n_taps):
    val_refs = refs[0:3 * n_cb:3]
    val_prev = refs[1:3 * n_cb:3]
    val_next = refs[2:3 * n_cb:3]
    gate_refs = refs[3 * n_cb:6 * n_cb:3]
    gate_prev = refs[3 * n_cb + 1:6 * n_cb:3]
    gate_next = refs[3 * n_cb + 2:6 * n_cb:3]
    dw_ref, dwb_ref, g_ref, b_ref, o_ref, hbuf_ref, shift_ref, conv_ref = refs[6 * n_cb:]
    i = pl.program_id(0)
    tm = o_ref.shape[0]
    has_prev = (i > 0).astype(F32)
    has_next = (i < pl.num_programs(0) - 1).astype(F32)
    pad = n_taps // 2
    base = _CONV_HALO - pad
    n_shift_rows = shift_ref.shape[1]

    def glu(vr, gr):
        return vr[...] * jax.nn.sigmoid(gr[...])

    for cb in range(n_cb):
        hbuf_ref[0:_CONV_HALO, :] = glu(val_prev[cb], gate_prev[cb]) * has_prev
        hbuf_ref[_CONV_HALO:_CONV_HALO + tm, :] = glu(val_refs[cb], gate_refs[cb])
        hbuf_ref[_CONV_HALO + tm:, :] = glu(val_next[cb], gate_next[cb]) * has_next
        for sh in range(SUBLANES):
            shift_ref[sh] = hbuf_ref[sh:sh + n_shift_rows, :]

        def row_chunk(rc, carry, cb=cb):
            r0 = pl.multiple_of(rc * _CONV_ROWS, _CONV_ROWS)
            for lc in range(cb_w // LANES):
                lanes = slice(lc * LANES, (lc + 1) * LANES)
                wl = slice(cb * cb_w + lc * LANES, cb * cb_w + (lc + 1) * LANES)
                acc = jnp.zeros((_CONV_ROWS, LANES), F32) + dwb_ref[:, wl]
                for kk in range(n_taps):
                    sh, blk = (base + kk) % SUBLANES, (base + kk) // SUBLANES
                    win = shift_ref[sh, pl.ds(r0 + blk * SUBLANES, _CONV_ROWS), lanes]
                    acc = acc + win * dw_ref[kk:kk + 1, wl]
                conv_ref[pl.ds(r0, _CONV_ROWS), wl] = acc
            return carry

        lax.fori_loop(0, tm // _CONV_ROWS, row_chunk, 0)

    y = _ln_rows(conv_ref[...], g_ref[...], b_ref[...])
    o_ref[...] = (y * jax.nn.sigmoid(y)).astype(o_ref.dtype)


def _conv_branch(z, col0, dw_w, dw_b, ln_g, ln_b, tm=256, cb_w=1024):
    t = z.shape[0]
    n_taps, width = dw_w.shape
    tm = min(tm, t)
    n_cb = width // cb_w
    assert col0 % cb_w == 0 and n_taps // 2 < _CONV_HALO and tm % _CONV_ROWS == 0
    hb = tm // _CONV_HALO
    last_hb = t // _CONV_HALO - 1

    def specs(first_block):
        out = []
        for cb in range(n_cb):
            c = first_block + cb
            out.append(pl.BlockSpec((tm, cb_w), lambda i, c=c: (i, c)))
            out.append(pl.BlockSpec((_CONV_HALO, cb_w), lambda i, c=c: (jnp.maximum(i * hb - 1, 0), c)))
            out.append(pl.BlockSpec((_CONV_HALO, cb_w), lambda i, c=c: (jnp.minimum((i + 1) * hb, last_hb), c)))
        return out

    vec = pl.BlockSpec((1, width), lambda i: (0, 0))
    in_specs = specs(col0 // cb_w) + specs((col0 + width) // cb_w)
    in_specs += [pl.BlockSpec((n_taps, width), lambda i: (0, 0)), vec, vec, vec]
    return pl.pallas_call(
        functools.partial(_conv_kernel, n_cb=n_cb, cb_w=cb_w, n_taps=n_taps),
        grid=(t // tm,),
        in_specs=in_specs,
        out_specs=pl.BlockSpec((tm, width), lambda i: (i, 0)),
        out_shape=jax.ShapeDtypeStruct((t, width), BF16),
        scratch_shapes=[pltpu.VMEM((tm + 2 * _CONV_HALO, cb_w), F32),
                        pltpu.VMEM((SUBLANES, tm + 2 * _CONV_HALO - SUBLANES, cb_w), F32),
                        pltpu.VMEM((tm, width), F32)],
        compiler_params=_params(("parallel",)),
        name="conv_branch",
    )(*([z] * (6 * n_cb)), dw_w, dw_b.reshape(1, width), ln_g.reshape(1, width), ln_b.reshape(1, width))


def _sgu_kernel(u_ref, v_ref, g_ref, b_ref, w_ref, bias_ref, o_ref):
    n_heads, chunk, _ = w_ref.shape
    hw = u_ref.shape[1] // n_heads
    for c in range(u_ref.shape[0] // chunk):
        rows = slice(c * chunk, (c + 1) * chunk)
        vn = _ln_rows(v_ref[rows, :], g_ref[...], b_ref[...]).astype(BF16)
        for hd in range(n_heads):
            cols = slice(hd * hw, (hd + 1) * hw)
            mixed = jnp.dot(w_ref[hd].astype(BF16), vn[:, cols], preferred_element_type=F32) + bias_ref[hd]
            o_ref[rows, cols] = (u_ref[rows, cols] * mixed).astype(o_ref.dtype)


def _sgu(g_act, ln_g, ln_b, sgu_w, sgu_b, tm=256):
    t, two_w = g_act.shape
    width = two_w // 2
    n_heads, chunk, _ = sgu_w.shape
    tm = min(tm, t)
    hw = width // n_heads
    bias = jnp.broadcast_to(sgu_b[:, :, None], (n_heads, chunk, hw))
    vec = pl.BlockSpec((1, width), lambda i: (0, 0))
    return pl.pallas_call(
        _sgu_kernel,
        grid=(t // tm,),
        in_specs=[pl.BlockSpec((tm, width), lambda i: (i, 0)),
                  pl.BlockSpec((tm, width), lambda i: (i, 1)),
                  vec, vec,
                  pl.BlockSpec((n_heads, chunk, chunk), lambda i: (0, 0, 0)),
                  pl.BlockSpec((n_heads, chunk, hw), lambda i: (0, 0, 0))],
        out_specs=pl.BlockSpec((tm, width), lambda i: (i, 0)),
        out_shape=jax.ShapeDtypeStruct((t, width), BF16),
        compiler_params=_params(("parallel",)),
        name="sgu",
    )(g_act, g_act, ln_g.reshape(1, width), ln_b.reshape(1, width), sgu_w, bias)


def _top_values(s, key, n, big_key):
    rows = []
    for _ in range(n):
        m = jnp.max(s, axis=0, keepdims=True)
        first = jnp.min(jnp.where(s == m, key, big_key), axis=0, keepdims=True)
        s = jnp.where(key == first, -jnp.inf, s)
        rows.append(m)
    return rows


def _route_kernel(q_ref, k1_ref, k2_ref, a_ref, c_ref, s2_ref, e2_ref, v1_ref, v2_ref):
    n_heads, n_keys, kd = k1_ref.shape
    tm = q_ref.shape[0]
    topk = PEER_TOPK
    half = topk // 2
    key_iota = lax.broadcasted_iota(jnp.int32, (n_keys, tm), 0)
    sub = lax.broadcasted_iota(jnp.int32, (SUBLANES, tm), 0)
    for hd in range(n_heads):
        q1 = q_ref[:, (2 * hd) * kd:(2 * hd + 1) * kd]
        q2 = q_ref[:, (2 * hd + 1) * kd:(2 * hd + 2) * kd]
        s1 = lax.dot_general(k1_ref[hd].astype(BF16), q1, _NT, preferred_element_type=F32)
        s2 = lax.dot_general(k2_ref[hd].astype(BF16), q2, _NT, preferred_element_type=F32)
        for r, row in enumerate(_top_values(s1, key_iota, topk, n_keys)):
            v1_ref[r:r + 1, :] = row
        for r, row in enumerate(_top_values(s2, key_iota, topk, n_keys)):
            v2_ref[r:r + 1, :] = row
        v2_lo, v2_hi = v2_ref[0:half, :], v2_ref[half:topk, :]
        cands = [v1_ref[0:1, :] + v2_lo, v1_ref[0:1, :] + v2_hi]
        keys = [sub, sub + half]
        for a in range(1, half):
            cands.append(v1_ref[a:a + 1, :] + v2_lo)
            keys.append(sub + a * topk)
        cands.append(v1_ref[half:topk, :] + v2_ref[0:1, :])
        keys.append((sub + half) * topk)
        cand = jnp.concatenate(cands, axis=0)
        ckey = jnp.concatenate(keys, axis=0)
        top = _top_values(cand, ckey, topk + 1, topk * topk)
        z = jnp.zeros_like(top[0])
        for r in range(topk):
            z = z + jnp.exp(top[r] - top[0])
        theta = 0.5 * (top[topk - 1] + top[topk])
        a_ref[hd] = jnp.exp(s1 - v1_ref[0:1, :]) / z
        c_ref[hd] = theta - s1
        s2_ref[hd] = s2
        e2_ref[hd] = jnp.exp(s2 - v2_ref[0:1, :])


def _route(q, k1, k2, tm=256):
    t = q.shape[0]
    n_heads, n_keys, kd = k1.shape
    tm = min(tm, t)
    kspec = pl.BlockSpec((n_heads, n_keys, kd), lambda i: (0, 0, 0))
    ospec = pl.BlockSpec((n_heads, n_keys, tm), lambda i: (0, 0, i))
    oshape = jax.ShapeDtypeStruct((n_heads, n_keys, t), F32)
    return pl.pallas_call(
        _route_kernel,
        grid=(t // tm,),
        in_specs=[pl.BlockSpec((tm, 2 * n_heads * kd), lambda i: (i, 0)), kspec, kspec],
        out_specs=[ospec] * 4,
        out_shape=[oshape] * 4,
        scratch_shapes=[pltpu.VMEM((PEER_TOPK, tm), F32), pltpu.VMEM((PEER_TOPK, tm), F32)],
        compiler_params=_params(("parallel",)),
        name="peer_route",
    )(q, k1, k2)


def _peer_kernel(h_ref, u_ref, v_ref, a_ref, c_ref, s2_ref, e2_ref, o_ref):
    j = pl.program_id(1)
    n_heads, n_keys, _ = s2_ref.shape
    n_i = u_ref.shape[0] // n_keys

    @pl.when(j == 0)
    def _():
        o_ref[...] = jnp.zeros(o_ref.shape, F32)

    act = jax.nn.gelu(lax.dot_general(u_ref[...], h_ref[...], _NT, preferred_element_type=F32))
    blocks = []
    for ii in range(n_i):
        i = j * n_i + ii
        gate = None
        for hd in range(n_heads):
            a_row = a_ref[hd, pl.ds(i, 1), :]
            c_row = c_ref[hd, pl.ds(i, 1), :]
            term = jnp.where(s2_ref[hd] >= c_row, a_row * e2_ref[hd], 0.0)
            gate = term if gate is None else gate + term
        blocks.append((gate * act[ii * n_keys:(ii + 1) * n_keys, :]).astype(BF16))
    w_t = jnp.concatenate(blocks, axis=0)
    o_ref[...] += lax.dot_general(w_t, v_ref[...], _TN, preferred_element_type=F32)


def _peer_dense(h, u_bf, v_bf, route, tm=512, te=512):
    t, d = h.shape
    n_exp = u_bf.shape[0]
    n_heads, n_keys, _ = route[0].shape
    tm = min(tm, t)
    assert te % n_keys == 0 and n_exp % te == 0
    once = dict(pipeline_mode=pl.Buffered(1))
    rspec = pl.BlockSpec((n_heads, n_keys, tm), lambda i, j: (0, 0, i), **once)
    return pl.pallas_call(
        _peer_kernel,
        grid=(t // tm, n_exp // te),
        in_specs=[pl.BlockSpec((tm, d), lambda i, j: (i, 0), **once),
                  pl.BlockSpec((te, d), lambda i, j: (j, 0)),
                  pl.BlockSpec((te, d), lambda i, j: (j, 0)),
                  rspec, rspec, rspec, rspec],
        out_specs=pl.BlockSpec((tm, d), lambda i, j: (i, 0)),
        out_shape=jax.ShapeDtypeStruct((t, d), F32),
        compiler_params=_params(("parallel", "arbitrary")),
        name="peer_dense",
    )(h, u_bf, v_bf, *route)


def _peer_ffn(h, w_q, k1, k2, u_tab, v_tab):
    q = _matmul([h], w_q, BF16, name="peer_wq")
    route = _route(q, k1, k2)
    return _peer_dense(h, _cast_bf16(u_tab), _cast_bf16(v_tab), route)


def _rope_tables(seq):
    n_freq = HEAD_DIM // 4
    inv = ROPE_THETA ** (-jnp.arange(n_freq, dtype=F32) / n_freq)
    pos = jnp.arange(seq, dtype=jnp.int32)
    ang_r = (pos // GRID_W).astype(F32)[:, None] * inv
    ang_c = (pos % GRID_W).astype(F32)[:, None] * inv
    cos_t = jnp.concatenate([jnp.cos(ang_r)] * 2 + [jnp.cos(ang_c)] * 2, axis=-1)
    sin_t = jnp.concatenate([-jnp.sin(ang_r), jnp.sin(ang_r), -jnp.sin(ang_c), jnp.sin(ang_c)], axis=-1)
    return cos_t, sin_t


def _attn_conv_mixer(h, hc, w_in, q_gain, k_gain, dw_w, dw_b, conv_ln_g, conv_ln_b, w_out):
    seq = h.shape[0]
    n_ctx = hc.shape[0]
    attn_w = KV_HEADS * GQA_GROUP * HEAD_DIM
    kv_w = KV_HEADS * HEAD_DIM
    n_q, n_k = attn_w // HEAD_DIM, kv_w // HEAD_DIM
    z = _matmul([h], w_in, F32, name="attn_w_in")
    kvc = _matmul([hc], w_in, F32, n_out=2 * kv_w, col_off=attn_w, name="ctx_kv")
    cos_t, sin_t = _rope_tables(seq)
    gains = jnp.concatenate([jnp.broadcast_to(q_gain, (n_q, HEAD_DIM)),
                             jnp.broadcast_to(k_gain, (n_k, HEAD_DIM))]).reshape(n_q + n_k, 1, HEAD_DIM)
    qk = _qk_prep(z, gains, cos_t, sin_t, n_q + n_k, n_q, HEAD_DIM ** -0.5)
    kc = _qk_prep(kvc, gains[n_q:], jnp.ones((n_ctx, HEAD_DIM), F32), jnp.zeros((n_ctx, HEAD_DIM), F32),
                  n_k, 0, 1.0)
    k_all = jnp.concatenate([kc, qk[:, attn_w:]], axis=0)
    v_all = jnp.concatenate([kvc[:, kv_w:], z[:, attn_w + kv_w:attn_w + 2 * kv_w]], axis=0).astype(BF16)
    o_attn = _flash(qk, k_all, v_all)
    o_conv = _conv_branch(z, attn_w + 2 * kv_w, dw_w, dw_b, conv_ln_g, conv_ln_b)
    return _matmul([o_attn, o_conv], w_out, F32, name="attn_w_out")


def _gmlp_mixer(h, w_in, sgu_ln_g, sgu_ln_b, sgu_w, sgu_b, w_out):
    g_act = _matmul([h], w_in, F32, act="gelu", name="gmlp_w_in")
    gated = _sgu(g_act, sgu_ln_g, sgu_ln_b, sgu_w, sgu_b)
    return _matmul([gated], w_out, F32, name="gmlp_w_out")


def kernel(x, c, ctx, c_ctx, l0_w_mod, l0_b_mod, l0_w_in, l0_q_gain, l0_k_gain, l0_dw_w, l0_dw_b, l0_conv_ln_g, l0_conv_ln_b, l0_w_out, l0_mix_ln_g, l0_mix_ln_b, l0_peer_wq, l0_peer_k1, l0_peer_k2, l0_peer_u, l0_peer_v, l0_ffn_ln_g, l0_ffn_ln_b, l1_w_mod, l1_b_mod, l1_w_in, l1_sgu_ln_g, l1_sgu_ln_b, l1_sgu_w, l1_sgu_b, l1_w_out, l1_mix_ln_g, l1_mix_ln_b, l1_peer_wq, l1_peer_k1, l1_peer_k2, l1_peer_u, l1_peer_v, l1_ffn_ln_g, l1_ffn_ln_b):
    b, s, d = x.shape
    assert b == 1 and c.shape[0] == 1 and ctx.shape[0] == 1
    xs = x.reshape(s, d)
    ctx2 = ctx.reshape(ctx.shape[1], d)
    c_cols = jnp.stack([c.reshape(d), c_ctx], axis=1)

    def mods(w_mod, b_mod):
        mod = _modvec(c_cols, w_mod, b_mod)
        return [mod[:, k * d:(k + 1) * d] for k in range(N_MOD)]

    m0 = mods(l0_w_mod, l0_b_mod)
    m1 = mods(l1_w_mod, l1_b_mod)
    shift_m, scale_m, gate_m, shift_f, scale_f, gate_f = [v[0:1] for v in m0]
    h = _modulate(xs, shift_m, scale_m)
    hc = _modulate(ctx2, m0[0][1:2], m0[1][1:2])
    y = _attn_conv_mixer(h, hc, l0_w_in, l0_q_gain, l0_k_gain, l0_dw_w, l0_dw_b,
                         l0_conv_ln_g, l0_conv_ln_b, l0_w_out)
    xs, h = _resln(xs, y, gate_m, l0_mix_ln_g, l0_mix_ln_b, shift_f, scale_f)
    y = _peer_ffn(h, l0_peer_wq, l0_peer_k1, l0_peer_k2, l0_peer_u, l0_peer_v)
    shift_m, scale_m, gate_m1, shift_f1, scale_f1, gate_f1 = [v[0:1] for v in m1]
    xs, h = _resln(xs, y, gate_f, l0_ffn_ln_g, l0_ffn_ln_b, shift_m, scale_m)

    y = _gmlp_mixer(h, l1_w_in, l1_sgu_ln_g, l1_sgu_ln_b, l1_sgu_w, l1_sgu_b, l1_w_out)
    xs, h = _resln(xs, y, gate_m1, l1_mix_ln_g, l1_mix_ln_b, shift_f1, scale_f1)
    y = _peer_ffn(h, l1_peer_wq, l1_peer_k1, l1_peer_k2, l1_peer_u, l1_peer_v)
    zero = jnp.zeros((1, d), F32)
    xs, _ = _resln(xs, y, gate_f1, l1_ffn_ln_g, l1_ffn_ln_b, zero, zero)
    return xs.reshape(b, s, d)
```

```python
import functools

import jax
import jax.numpy as jnp
from jax import lax
from jax.experimental import pallas as pl
from jax.experimental.pallas import tpu as pltpu

F32 = jnp.float32
BF16 = jnp.bfloat16

GRID_W = 64
HEAD_DIM = 128
KV_HEADS = 4
GQA_GROUP = 4
ROPE_THETA = 10000.0
PEER_TOPK = 16
LN_EPS = 1e-6
DEPTH = 2
DEEPNORM_ALPHA = (2 * DEPTH) ** 0.25
N_MOD = 6
LOG2_E = 1.4426950408889634

LANES = 128
SUBLANES = 8
VMEM_LIMIT = 56 * 1024 * 1024

_NT = (((1,), (1,)), ((), ()))
_TN = (((0,), (0,)), ((), ()))


def _params(sem):
    return pltpu.CompilerParams(dimension_semantics=sem, vmem_limit_bytes=VMEM_LIMIT)


def _ln_rows(x, g, b):
    mu = jnp.mean(x, axis=-1, keepdims=True)
    xc = x - mu
    var = jnp.mean(xc * xc, axis=-1, keepdims=True)
    return xc * lax.rsqrt(var + LN_EPS) * g + b


def _modvec_kernel(c_ref, w_ref, b_ref, o_ref, sb_ref):
    n_vec = sb_ref.shape[0]

    @pl.when(pl.program_id(0) == 0)
    def _():
        cv = c_ref[...]
        s = cv * jax.nn.sigmoid(cv)
        for r in range(n_vec):
            sb_ref[r] = jnp.broadcast_to(s[:, r:r + 1], sb_ref.shape[1:])

    tn = w_ref.shape[1]
    for j in range(tn // LANES):
        cols = slice(j * LANES, (j + 1) * LANES)
        w = w_ref[:, cols]
        for r in range(n_vec):
            o_ref[r:r + 1, cols] = jnp.sum(w * sb_ref[r], axis=0, keepdims=True) + b_ref[:, cols]


def _modvec(c_cols, w_mod, b_mod, tn=512):
    d, n_vec = c_cols.shape
    n = w_mod.shape[1]
    return pl.pallas_call(
        _modvec_kernel,
        grid=(n // tn,),
        in_specs=[pl.BlockSpec((d, n_vec), lambda j: (0, 0)),
                  pl.BlockSpec((d, tn), lambda j: (0, j)),
                  pl.BlockSpec((1, tn), lambda j: (0, j))],
        out_specs=pl.BlockSpec((n_vec, tn), lambda j: (0, j)),
        out_shape=jax.ShapeDtypeStruct((n_vec, n), F32),
        scratch_shapes=[pltpu.VMEM((n_vec, d, LANES), F32)],
        compiler_params=_params(("arbitrary",)),
        name="modvec",
    )(c_cols, w_mod, b_mod.reshape(1, n))


def _modulate_kernel(x_ref, shift_ref, scale_ref, o_ref):
    o_ref[...] = (x_ref[...] * (1.0 + scale_ref[...]) + shift_ref[...]).astype(o_ref.dtype)


def _modulate(x, shift, scale, tm=256):
    t, d = x.shape
    tm = min(tm, t)
    vec = pl.BlockSpec((1, d), lambda i: (0, 0))
    return pl.pallas_call(
        _modulate_kernel,
        grid=(t // tm,),
        in_specs=[pl.BlockSpec((tm, d), lambda i: (i, 0)), vec, vec],
        out_specs=pl.BlockSpec((tm, d), lambda i: (i, 0)),
        out_shape=jax.ShapeDtypeStruct((t, d), BF16),
        compiler_params=_params(("parallel",)),
        name="modulate",
    )(x, shift, scale)


def _cast_kernel(x_ref, o_ref):
    o_ref[...] = x_ref[...].astype(o_ref.dtype)


def _cast_bf16(x, tm=512):
    r, c = x.shape
    return pl.pallas_call(
        _cast_kernel,
        grid=(r // tm,),
        in_specs=[pl.BlockSpec((tm, c), lambda i: (i, 0))],
        out_specs=pl.BlockSpec((tm, c), lambda i: (i, 0)),
        out_shape=jax.ShapeDtypeStruct((r, c), BF16),
        compiler_params=_params(("parallel",)),
        name="cast_bf16",
    )(x)


def _matmul_kernel(*refs, n_a, act):
    a_refs, w_ref, o_ref = refs[:n_a], refs[n_a], refs[n_a + 1]
    w = w_ref[...].astype(BF16)
    acc = None
    k0 = 0
    for a_ref in a_refs:
        k1 = k0 + a_ref.shape[1]
        part = jnp.dot(a_ref[...], w[k0:k1], preferred_element_type=F32)
        acc = part if acc is None else acc + part
        k0 = k1
    if act == "gelu":
        acc = jax.nn.gelu(acc)
    o_ref[...] = acc.astype(o_ref.dtype)


def _matmul(a_list, w, out_dtype, *, n_out=None, col_off=0, act=None, bm=1024, bn=512, name="matmul"):
    m = a_list[0].shape[0]
    k = w.shape[0]
    n_out = w.shape[1] if n_out is None else n_out
    bm = min(bm, m)
    assert sum(a.shape[1] for a in a_list) == k
    assert m % bm == 0 and n_out % bn == 0 and col_off % bn == 0
    off_blocks = col_off // bn
    in_specs = [pl.BlockSpec((bm, a.shape[1]), lambda i, j: (i, 0)) for a in a_list]
    in_specs.append(pl.BlockSpec((k, bn), lambda i, j: (0, j + off_blocks)))
    return pl.pallas_call(
        functools.partial(_matmul_kernel, n_a=len(a_list), act=act),
        grid=(m // bm, n_out // bn),
        in_specs=in_specs,
        out_specs=pl.BlockSpec((bm, bn), lambda i, j: (i, j)),
        out_shape=jax.ShapeDtypeStruct((m, n_out), out_dtype),
        compiler_params=_params(("parallel", "parallel")),
        name=name,
    )(*a_list, w)


def _resln_kernel(x_ref, y_ref, gate_ref, g_ref, b_ref, shift_ref, scale_ref, xo_ref, ho_ref, *hto_ref):
    r = DEEPNORM_ALPHA * x_ref[...] + gate_ref[...] * y_ref[...]
    xn = _ln_rows(r, g_ref[...], b_ref[...])
    xo_ref[...] = xn
    h = xn * (1.0 + scale_ref[...]) + shift_ref[...]
    ho_ref[...] = h.astype(ho_ref.dtype)
    if hto_ref:
        hto_ref[0][...] = h.T.astype(hto_ref[0].dtype)


def _resln(x, y, gate, g, b, shift, scale, transposed=False, tm=256):
    t, d = x.shape
    tm = min(tm, t)
    row = pl.BlockSpec((tm, d), lambda i: (i, 0))
    vec = pl.BlockSpec((1, d), lambda i: (0, 0))
    out_specs = [row, row]
    out_shape = [jax.ShapeDtypeStruct((t, d), F32), jax.ShapeDtypeStruct((t, d), BF16)]
    if transposed:
        out_specs.append(pl.BlockSpec((d, tm), lambda i: (0, i)))
        out_shape.append(jax.ShapeDtypeStruct((d, t), BF16))
    return pl.pallas_call(
        _resln_kernel,
        grid=(t // tm,),
        in_specs=[row, row, vec, vec, vec, vec, vec],
        out_specs=out_specs,
        out_shape=out_shape,
        compiler_params=_params(("parallel",)),
        name="resln",
    )(x, y, gate, g.reshape(1, d), b.reshape(1, d), shift, scale)


def _qk_prep_kernel(z_ref, gain_ref, cos_ref, sin_ref, o_ref, *, n_q_blocks, q_scale):
    x = z_ref[...]
    x = x * lax.rsqrt(jnp.mean(x * x, axis=-1, keepdims=True) + LN_EPS) * gain_ref[0]
    quarter = HEAD_DIM // 4
    lane = lax.broadcasted_iota(jnp.int32, x.shape, 1)
    partner = jnp.where((lane & quarter) == 0,
                        pltpu.roll(x, HEAD_DIM - quarter, axis=1),
                        pltpu.roll(x, quarter, axis=1))
    y = x * cos_ref[...] + partner * sin_ref[...]
    y = y * jnp.where(pl.program_id(1) < n_q_blocks, q_scale, 1.0)
    o_ref[...] = y.astype(o_ref.dtype)


def _qk_prep(z, gains, cos_t, sin_t, n_blocks, n_q_blocks, q_scale, tm=512):
    t = z.shape[0]
    tm = min(tm, t)
    return pl.pallas_call(
        functools.partial(_qk_prep_kernel, n_q_blocks=n_q_blocks, q_scale=q_scale),
        grid=(t // tm, n_blocks),
        in_specs=[pl.BlockSpec((tm, HEAD_DIM), lambda i, j: (i, j)),
                  pl.BlockSpec((1, 1, HEAD_DIM), lambda i, j: (j, 0, 0)),
                  pl.BlockSpec((tm, HEAD_DIM), lambda i, j: (i, 0)),
                  pl.BlockSpec((tm, HEAD_DIM), lambda i, j: (i, 0))],
        out_specs=pl.BlockSpec((tm, HEAD_DIM), lambda i, j: (i, j)),
        out_shape=jax.ShapeDtypeStruct((t, n_blocks * HEAD_DIM), BF16),
        compiler_params=_params(("parallel", "arbitrary")),
        name="qk_prep",
    )(z, gains, cos_t, sin_t)


def _flash_kernel(q_ref, k_ref, v_ref, o_ref, m_ref, l_ref, acc_ref):
    kt = pl.program_id(2)

    @pl.when(kt == 0)
    def _():
        m_ref[...] = jnp.full(m_ref.shape, -jnp.inf, F32)
        l_ref[...] = jnp.zeros(l_ref.shape, F32)
        acc_ref[...] = jnp.zeros(acc_ref.shape, F32)

    k = k_ref[...]
    v = v_ref[...]
    for g in range(GQA_GROUP):
        cols = slice(g * HEAD_DIM, (g + 1) * HEAD_DIM)
        s_t = lax.dot_general(k, q_ref[:, cols], _NT, preferred_element_type=F32)
        m_prev = m_ref[g]
        m_new = jnp.maximum(m_prev, jnp.max(s_t, axis=0, keepdims=True))
        alpha = jnp.exp2(m_prev - m_new)
        p = jnp.exp2(s_t - m_new)
        l_ref[g] = alpha * l_ref[g] + jnp.sum(p, axis=0, keepdims=True)
        acc_ref[g] = alpha * acc_ref[g] + lax.dot_general(
            v, p.astype(v.dtype), _TN, preferred_element_type=F32)
        m_ref[g] = m_new

    @pl.when(kt == pl.num_programs(2) - 1)
    def _():
        for g in range(GQA_GROUP):
            cols = slice(g * HEAD_DIM, (g + 1) * HEAD_DIM)
            o_ref[:, cols] = (acc_ref[g] / l_ref[g]).T.astype(o_ref.dtype)


def _flash(q, k_all, v_all, tq=1024, tk=768):
    s_q = q.shape[0]
    s_k = k_all.shape[0]
    tq = min(tq, s_q)
    if s_k % tk:
        tk = 256
    gw = GQA_GROUP * HEAD_DIM
    return pl.pallas_call(
        _flash_kernel,
        grid=(KV_HEADS, s_q // tq, s_k // tk),
        in_specs=[pl.BlockSpec((tq, gw), lambda h, i, j: (i, h)),
                  pl.BlockSpec((tk, HEAD_DIM), lambda h, i, j: (j, h)),
                  pl.BlockSpec((tk, HEAD_DIM), lambda h, i, j: (j, h))],
        out_specs=pl.BlockSpec((tq, gw), lambda h, i, j: (i, h)),
        out_shape=jax.ShapeDtypeStruct((s_q, KV_HEADS * gw), BF16),
        scratch_shapes=[pltpu.VMEM((GQA_GROUP, 1, tq), F32),
                        pltpu.VMEM((GQA_GROUP, 1, tq), F32),
                        pltpu.VMEM((GQA_GROUP, HEAD_DIM, tq), F32)],
        compiler_params=_params(("parallel", "parallel", "arbitrary")),
        name="flash",
    )(q, k_all, v_all)


_CONV_HALO = 16
_CONV_ROWS = 64


def _conv_kernel(*refs, n_cb, cb_w, n_taps):
    val_refs = refs[0:3 * n_cb:3]
    val_prev = refs[1:3 * n_cb:3]
    val_next = refs[2:3 * n_cb:3]
    gate_refs = refs[3 * n_cb:6 * n_cb:3]
    gate_prev = refs[3 * n_cb + 1:6 * n_cb:3]
    gate_next = refs[3 * n_cb + 2:6 * n_cb:3]
    dw_ref, dwb_ref, g_ref, b_ref, o_ref, hbuf_ref, shift_ref, conv_ref = refs[6 * n_cb:]
    i = pl.program_id(0)
    tm = o_ref.shape[0]
    has_prev = (i > 0).astype(F32)
    has_next = (i < pl.num_programs(0) - 1).astype(F32)
    pad = n_taps // 2
    base = _CONV_HALO - pad
    n_shift_rows = shift_ref.shape[1]

    def glu(vr, gr):
        return vr[...] * jax.nn.sigmoid(gr[...])

    for cb in range(n_cb):
        hbuf_ref[0:_CONV_HALO, :] = glu(val_prev[cb], gate_prev[cb]) * has_prev
        hbuf_ref[_CONV_HALO:_CONV_HALO + tm, :] = glu(val_refs[cb], gate_refs[cb])
        hbuf_ref[_CONV_HALO + tm:, :] = glu(val_next[cb], gate_next[cb]) * has_next
        for sh in range(SUBLANES):
            shift_ref[sh] = hbuf_ref[sh:sh + n_shift_rows, :]

        def row_chunk(rc, carry, cb=cb):
            r0 = pl.multiple_of(rc * _CONV_ROWS, _CONV_ROWS)
            for lc in range(cb_w // LANES):
                lanes = slice(lc * LANES, (lc + 1) * LANES)
                wl = slice(cb * cb_w + lc * LANES, cb * cb_w + (lc + 1) * LANES)
                acc = jnp.zeros((_CONV_ROWS, LANES), F32) + dwb_ref[:, wl]
                for kk in range(n_taps):
                    sh, blk = (base + kk) % SUBLANES, (base + kk) // SUBLANES
                    win = shift_ref[sh, pl.ds(r0 + blk * SUBLANES, _CONV_ROWS), lanes]
                    acc = acc + win * dw_ref[kk:kk + 1, wl]
                conv_ref[pl.ds(r0, _CONV_ROWS), wl] = acc
            return carry

        lax.fori_loop(0, tm // _CONV_ROWS, row_chunk, 0)

    y = _ln_rows(conv_ref[...], g_ref[...], b_ref[...])
    o_ref[...] = (y * jax.nn.sigmoid(y)).astype(o_ref.dtype)


def _conv_branch(z, col0, dw_w, dw_b, ln_g, ln_b, tm=256, cb_w=1024):
    t = z.shape[0]
    n_taps, width = dw_w.shape
    tm = min(tm, t)
    n_cb = width // cb_w
    assert col0 % cb_w == 0 and n_taps // 2 < _CONV_HALO and tm % _CONV_ROWS == 0
    hb = tm // _CONV_HALO
    last_hb = t // _CONV_HALO - 1

    def specs(first_block):
        out = []
        for cb in range(n_cb):
            c = first_block + cb
            out.append(pl.BlockSpec((tm, cb_w), lambda i, c=c: (i, c)))
            out.append(pl.BlockSpec((_CONV_HALO, cb_w), lambda i, c=c: (jnp.maximum(i * hb - 1, 0), c)))
            out.append(pl.BlockSpec((_CONV_HALO, cb_w), lambda i, c=c: (jnp.minimum((i + 1) * hb, last_hb), c)))
        return out

    vec = pl.BlockSpec((1, width), lambda i: (0, 0))
    in_specs = specs(col0 // cb_w) + specs((col0 + width) // cb_w)
    in_specs += [pl.BlockSpec((n_taps, width), lambda i: (0, 0)), vec, vec, vec]
    return pl.pallas_call(
        functools.partial(_conv_kernel, n_cb=n_cb, cb_w=cb_w, n_taps=n_taps),
        grid=(t // tm,),
        in_specs=in_specs,
        out_specs=pl.BlockSpec((tm, width), lambda i: (i, 0)),
        out_shape=jax.ShapeDtypeStruct((t, width), BF16),
        scratch_shapes=[pltpu.VMEM((tm + 2 * _CONV_HALO, cb_w), F32),
                        pltpu.VMEM((SUBLANES, tm + 2 * _CONV_HALO - SUBLANES, cb_w), F32),
                        pltpu.VMEM((tm, width), F32)],
        compiler_params=_params(("parallel",)),
        name="conv_branch",
    )(*([z] * (6 * n_cb)), dw_w, dw_b.reshape(1, width), ln_g.reshape(1, width), ln_b.reshape(1, width))


def _sgu_kernel(u_ref, v_ref, g_ref, b_ref, w_ref, bias_ref, o_ref):
    n_heads, chunk, _ = w_ref.shape
    hw = u_ref.shape[1] // n_heads
    for c in range(u_ref.shape[0] // chunk):
        rows = slice(c * chunk, (c + 1) * chunk)
        vn = _ln_rows(v_ref[rows, :], g_ref[...], b_ref[...]).astype(BF16)
        for hd in range(n_heads):
            cols = slice(hd * hw, (hd + 1) * hw)
            mixed = jnp.dot(w_ref[hd].astype(BF16), vn[:, cols], preferred_element_type=F32) + bias_ref[hd]
            o_ref[rows, cols] = (u_ref[rows, cols] * mixed).astype(o_ref.dtype)


def _sgu(g_act, ln_g, ln_b, sgu_w, sgu_b, tm=256):
    t, two_w = g_act.shape
    width = two_w // 2
    n_heads, chunk, _ = sgu_w.shape
    tm = min(tm, t)
    hw = width // n_heads
    bias = jnp.broadcast_to(sgu_b[:, :, None], (n_heads, chunk, hw))
    vec = pl.BlockSpec((1, width), lambda i: (0, 0))
    return pl.pallas_call(
        _sgu_kernel,
        grid=(t // tm,),
        in_specs=[pl.BlockSpec((tm, width), lambda i: (i, 0)),
                  pl.BlockSpec((tm, width), lambda i: (i, 1)),
                  vec, vec,
                  pl.BlockSpec((n_heads, chunk, chunk), lambda i: (0, 0, 0)),
                  pl.BlockSpec((n_heads, chunk, hw), lambda i: (0, 0, 0))],
        out_specs=pl.BlockSpec((tm, width), lambda i: (i, 0)),
        out_shape=jax.ShapeDtypeStruct((t, width), BF16),
        compiler_params=_params(("parallel",)),
        name="sgu",
    )(g_act, g_act, ln_g.reshape(1, width), ln_b.reshape(1, width), sgu_w, bias)


def _top_values(s, key, n, big_key):
    rows = []
    for _ in range(n):
        m = jnp.max(s, axis=0, keepdims=True)
        first = jnp.min(jnp.where(s == m, key, big_key), axis=0, keepdims=True)
        s = jnp.where(key == first, -jnp.inf, s)
        rows.append(m)
    return rows


def _route_kernel(q_ref, k1_ref, k2_ref, a_ref, c_ref, s2_ref, e2_ref, v1_ref, v2_ref):
    n_heads, n_keys, kd = k1_ref.shape
    tm = q_ref.shape[0]
    topk = PEER_TOPK
    half = topk // 2
    key_iota = lax.broadcasted_iota(jnp.int32, (n_keys, tm), 0)
    sub = lax.broadcasted_iota(jnp.int32, (SUBLANES, tm), 0)
    for hd in range(n_heads):
        q1 = q_ref[:, (2 * hd) * kd:(2 * hd + 1) * kd]
        q2 = q_ref[:, (2 * hd + 1) * kd:(2 * hd + 2) * kd]
        s1 = lax.dot_general(k1_ref[hd].astype(BF16), q1, _NT, preferred_element_type=F32)
        s2 = lax.dot_general(k2_ref[hd].astype(BF16), q2, _NT, preferred_element_type=F32)
        for r, row in enumerate(_top_values(s1, key_iota, topk, n_keys)):
            v1_ref[r:r + 1, :] = row
        for r, row in enumerate(_top_values(s2, key_iota, topk, n_keys)):
            v2_ref[r:r + 1, :] = row
        v2_lo, v2_hi = v2_ref[0:half, :], v2_ref[half:topk, :]
        cands = [v1_ref[0:1, :] + v2_lo, v1_ref[0:1, :] + v2_hi]
        keys = [sub, sub + half]
        for a in range(1, half):
            cands.append(v1_ref[a:a + 1, :] + v2_lo)
            keys.append(sub + a * topk)
        cands.append(v1_ref[half:topk, :] + v2_ref[0:1, :])
        keys.append((sub + half) * topk)
        cand = jnp.concatenate(cands, axis=0)
        ckey = jnp.concatenate(keys, axis=0)
        top = _top_values(cand, ckey, topk + 1, topk * topk)
        z = jnp.zeros_like(top[0])
        for r in range(topk):
            z = z + jnp.exp(top[r] - top[0])
        theta = 0.5 * (top[topk - 1] + top[topk])
        a_ref[hd] = jnp.exp(s1 - v1_ref[0:1, :]) / z
        c_ref[hd] = theta - s1
        s2_ref[hd] = s2
        e2_ref[hd] = jnp.exp(s2 - v2_ref[0:1, :])


def _route(q, k1, k2, tm=256):
    t = q.shape[0]
    n_heads, n_keys, kd = k1.shape
    tm = min(tm, t)
    kspec = pl.BlockSpec((n_heads, n_keys, kd), lambda i: (0, 0, 0))
    ospec = pl.BlockSpec((n_heads, n_keys, tm), lambda i: (0, 0, i))
    oshape = jax.ShapeDtypeStruct((n_heads, n_keys, t), F32)
    return pl.pallas_call(
        _route_kernel,
        grid=(t // tm,),
        in_specs=[pl.BlockSpec((tm, 2 * n_heads * kd), lambda i: (i, 0)), kspec, kspec],
        out_specs=[ospec] * 4,
        out_shape=[oshape] * 4,
        scratch_shapes=[pltpu.VMEM((PEER_TOPK, tm), F32), pltpu.VMEM((PEER_TOPK, tm), F32)],
        compiler_params=_params(("parallel",)),
        name="peer_route",
    )(q, k1, k2)


def _peer_kernel(ht_ref, u_ref, v_ref, a_ref, c_ref, s2_ref, e2_ref, o_ref, act0_ref, act1_ref, *, n_tiles):
    j = pl.program_id(1)
    n_heads, n_keys, _ = s2_ref.shape
    n_i = u_ref.shape[0] // n_keys
    act_refs = (act0_ref, act1_ref)

    def first_matmul(slot):
        act_refs[slot][...] = jnp.dot(u_ref[...], ht_ref[...], preferred_element_type=F32)

    def second_matmul(slot):
        act = jax.nn.gelu(act_refs[slot][...])
        blocks = []
        for ii in range(n_i):
            i = (j - 1) * n_i + ii
            gate = None
            for hd in range(n_heads):
                a_row = a_ref[hd, pl.ds(i, 1), :]
                c_row = c_ref[hd, pl.ds(i, 1), :]
                term = jnp.where(s2_ref[hd] >= c_row, a_row * e2_ref[hd], 0.0)
                gate = term if gate is None else gate + term
            blocks.append((gate * act[ii * n_keys:(ii + 1) * n_keys, :]).astype(BF16))
        w_t = jnp.concatenate(blocks, axis=0)
        o_ref[...] += lax.dot_general(w_t, v_ref[...], _TN, preferred_element_type=F32)

    @pl.when(j == 0)
    def _():
        o_ref[...] = jnp.zeros(o_ref.shape, F32)
        first_matmul(0)

    for parity in (0, 1):
        @pl.when((j > 0) & (j < n_tiles) & (j % 2 == parity))
        def _(parity=parity):
            first_matmul(parity)
            second_matmul(1 - parity)

    @pl.when(j == n_tiles)
    def _():
        second_matmul((n_tiles - 1) % 2)


def _peer_dense(h_t, u_bf, v_bf, route, tm=512, te=512):
    d, t = h_t.shape
    n_exp = u_bf.shape[0]
    n_heads, n_keys, _ = route[0].shape
    tm = min(tm, t)
    assert te % n_keys == 0 and n_exp % te == 0
    n_tiles = n_exp // te
    once = dict(pipeline_mode=pl.Buffered(1))
    rspec = pl.BlockSpec((n_heads, n_keys, tm), lambda i, j: (0, 0, i), **once)
    return pl.pallas_call(
        functools.partial(_peer_kernel, n_tiles=n_tiles),
        grid=(t // tm, n_tiles + 1),
        in_specs=[pl.BlockSpec((d, tm), lambda i, j: (0, i), **once),
                  pl.BlockSpec((te, d), lambda i, j: (jnp.minimum(j, n_tiles - 1), 0)),
                  pl.BlockSpec((te, d), lambda i, j: (jnp.maximum(j - 1, 0), 0)),
                  rspec, rspec, rspec, rspec],
        out_specs=pl.BlockSpec((tm, d), lambda i, j: (i, 0)),
        out_shape=jax.ShapeDtypeStruct((t, d), F32),
        scratch_shapes=[pltpu.VMEM((te, tm), F32), pltpu.VMEM((te, tm), F32)],
        compiler_params=_params(("parallel", "arbitrary")),
        name="peer_dense",
    )(h_t, u_bf, v_bf, *route)


def _peer_ffn(h, h_t, w_q, k1, k2, u_tab, v_tab):
    q = _matmul([h], w_q, BF16, name="peer_wq")
    route = _route(q, k1, k2)
    return _peer_dense(h_t, _cast_bf16(u_tab), _cast_bf16(v_tab), route)


def _rope_tables(seq):
    n_freq = HEAD_DIM // 4
    inv = ROPE_THETA ** (-jnp.arange(n_freq, dtype=F32) / n_freq)
    pos = jnp.arange(seq, dtype=jnp.int32)
    ang_r = (pos // GRID_W).astype(F32)[:, None] * inv
    ang_c = (pos % GRID_W).astype(F32)[:, None] * inv
    cos_t = jnp.concatenate([jnp.cos(ang_r)] * 2 + [jnp.cos(ang_c)] * 2, axis=-1)
    sin_t = jnp.concatenate([-jnp.sin(ang_r), jnp.sin(ang_r), -jnp.sin(ang_c), jnp.sin(ang_c)], axis=-1)
    return cos_t, sin_t


def _attn_conv_mixer(h, hc, w_in, q_gain, k_gain, dw_w, dw_b, conv_ln_g, conv_ln_b, w_out):
    seq = h.shape[0]
    n_ctx = hc.shape[0]
    attn_w = KV_HEADS * GQA_GROUP * HEAD_DIM
    kv_w = KV_HEADS * HEAD_DIM
    n_q, n_k = attn_w // HEAD_DIM, kv_w // HEAD_DIM
    z = _matmul([h], w_in, F32, name="attn_w_in")
    kvc = _matmul([hc], w_in, F32, n_out=2 * kv_w, col_off=attn_w, name="ctx_kv")
    cos_t, sin_t = _rope_tables(seq)
    gains = jnp.concatenate([jnp.broadcast_to(q_gain, (n_q, HEAD_DIM)),
                             jnp.broadcast_to(k_gain, (n_k, HEAD_DIM))]).reshape(n_q + n_k, 1, HEAD_DIM)
    qk = _qk_prep(z, gains, cos_t, sin_t, n_q + n_k, n_q, HEAD_DIM ** -0.5 * LOG2_E)
    kc = _qk_prep(kvc, gains[n_q:], jnp.ones((n_ctx, HEAD_DIM), F32), jnp.zeros((n_ctx, HEAD_DIM), F32),
                  n_k, 0, 1.0)
    k_all = jnp.concatenate([kc, qk[:, attn_w:]], axis=0)
    v_all = jnp.concatenate([kvc[:, kv_w:], z[:, attn_w + kv_w:attn_w + 2 * kv_w]], axis=0).astype(BF16)
    o_attn = _flash(qk, k_all, v_all)
    o_conv = _conv_branch(z, attn_w + 2 * kv_w, dw_w, dw_b, conv_ln_g, conv_ln_b)
    return _matmul([o_attn, o_conv], w_out, F32, name="attn_w_out")


def _gmlp_mixer(h, w_in, sgu_ln_g, sgu_ln_b, sgu_w, sgu_b, w_out):
    g_act = _matmul([h], w_in, F32, act="gelu", name="gmlp_w_in")
    gated = _sgu(g_act, sgu_ln_g, sgu_ln_b, sgu_w, sgu_b)
    return _matmul([gated], w_out, F32, name="gmlp_w_out")


def kernel(x, c, ctx, c_ctx, l0_w_mod, l0_b_mod, l0_w_in, l0_q_gain, l0_k_gain, l0_dw_w, l0_dw_b, l0_conv_ln_g, l0_conv_ln_b, l0_w_out, l0_mix_ln_g, l0_mix_ln_b, l0_peer_wq, l0_peer_k1, l0_peer_k2, l0_peer_u, l0_peer_v, l0_ffn_ln_g, l0_ffn_ln_b, l1_w_mod, l1_b_mod, l1_w_in, l1_sgu_ln_g, l1_sgu_ln_b, l1_sgu_w, l1_sgu_b, l1_w_out, l1_mix_ln_g, l1_mix_ln_b, l1_peer_wq, l1_peer_k1, l1_peer_k2, l1_peer_u, l1_peer_v, l1_ffn_ln_g, l1_ffn_ln_b):
    b, s, d = x.shape
    assert b == 1 and c.shape[0] == 1 and ctx.shape[0] == 1
    xs = x.reshape(s, d)
    ctx2 = ctx.reshape(ctx.shape[1], d)
    c_cols = jnp.stack([c.reshape(d), c_ctx], axis=1)

    def mods(w_mod, b_mod):
        mod = _modvec(c_cols, w_mod, b_mod)
        return [mod[:, k * d:(k + 1) * d] for k in range(N_MOD)]

    m0 = mods(l0_w_mod, l0_b_mod)
    m1 = mods(l1_w_mod, l1_b_mod)
    shift_m, scale_m, gate_m, shift_f, scale_f, gate_f = [v[0:1] for v in m0]
    h = _modulate(xs, shift_m, scale_m)
    hc = _modulate(ctx2, m0[0][1:2], m0[1][1:2])
    y = _attn_conv_mixer(h, hc, l0_w_in, l0_q_gain, l0_k_gain, l0_dw_w, l0_dw_b,
                         l0_conv_ln_g, l0_conv_ln_b, l0_w_out)
    xs, h, h_t = _resln(xs, y, gate_m, l0_mix_ln_g, l0_mix_ln_b, shift_f, scale_f, transposed=True)
    y = _peer_ffn(h, h_t, l0_peer_wq, l0_peer_k1, l0_peer_k2, l0_peer_u, l0_peer_v)
    shift_m, scale_m, gate_m1, shift_f1, scale_f1, gate_f1 = [v[0:1] for v in m1]
    xs, h = _resln(xs, y, gate_f, l0_ffn_ln_g, l0_ffn_ln_b, shift_m, scale_m)

    y = _gmlp_mixer(h, l1_w_in, l1_sgu_ln_g, l1_sgu_ln_b, l1_sgu_w, l1_sgu_b, l1_w_out)
    xs, h, h_t = _resln(xs, y, gate_m1, l1_mix_ln_g, l1_mix_ln_b, shift_f1, scale_f1, transposed=True)
    y = _peer_ffn(h, h_t, l1_peer_wq, l1_peer_k1, l1_peer_k2, l1_peer_u, l1_peer_v)
    zero = jnp.zeros((1, d), F32)
    xs, _ = _resln(xs, y, gate_f1, l1_ffn_ln_g, l1_ffn_ln_b, zero, zero)
    return xs.reshape(b, s, d)
```

```python
import functools

import jax
import jax.numpy as jnp
from jax import lax
from jax.experimental import pallas as pl
from jax.experimental.pallas import tpu as pltpu

F32 = jnp.float32
BF16 = jnp.bfloat16

GRID_W = 64
HEAD_DIM = 128
KV_HEADS = 4
GQA_GROUP = 4
ROPE_THETA = 10000.0
PEER_TOPK = 16
LN_EPS = 1e-6
DEPTH = 2
DEEPNORM_ALPHA = (2 * DEPTH) ** 0.25
N_MOD = 6
LOG2_E = 1.4426950408889634

LANES = 128
SUBLANES = 8
VMEM_LIMIT = 56 * 1024 * 1024

_NT = (((1,), (1,)), ((), ()))
_TN = (((0,), (0,)), ((), ()))


def _params(sem):
    return pltpu.CompilerParams(dimension_semantics=sem, vmem_limit_bytes=VMEM_LIMIT)


def _ln_rows(x, g, b):
    mu = jnp.mean(x, axis=-1, keepdims=True)
    xc = x - mu
    var = jnp.mean(xc * xc, axis=-1, keepdims=True)
    return xc * lax.rsqrt(var + LN_EPS) * g + b


def _modvec_kernel(c_ref, w_ref, b_ref, o_ref, sb_ref):
    n_vec = sb_ref.shape[0]

    @pl.when(pl.program_id(0) == 0)
    def _():
        cv = c_ref[...]
        s = cv * jax.nn.sigmoid(cv)
        for r in range(n_vec):
            sb_ref[r] = jnp.broadcast_to(s[:, r:r + 1], sb_ref.shape[1:])

    tn = w_ref.shape[1]
    for j in range(tn // LANES):
        cols = slice(j * LANES, (j + 1) * LANES)
        w = w_ref[:, cols]
        for r in range(n_vec):
            o_ref[r:r + 1, cols] = jnp.sum(w * sb_ref[r], axis=0, keepdims=True) + b_ref[:, cols]


def _modvec(c_cols, w_mod, b_mod, tn=512):
    d, n_vec = c_cols.shape
    n = w_mod.shape[1]
    return pl.pallas_call(
        _modvec_kernel,
        grid=(n // tn,),
        in_specs=[pl.BlockSpec((d, n_vec), lambda j: (0, 0)),
                  pl.BlockSpec((d, tn), lambda j: (0, j)),
                  pl.BlockSpec((1, tn), lambda j: (0, j))],
        out_specs=pl.BlockSpec((n_vec, tn), lambda j: (0, j)),
        out_shape=jax.ShapeDtypeStruct((n_vec, n), F32),
        scratch_shapes=[pltpu.VMEM((n_vec, d, LANES), F32)],
        compiler_params=_params(("arbitrary",)),
        name="modvec",
    )(c_cols, w_mod, b_mod.reshape(1, n))


def _modulate_kernel(x_ref, shift_ref, scale_ref, o_ref):
    o_ref[...] = (x_ref[...] * (1.0 + scale_ref[...]) + shift_ref[...]).astype(o_ref.dtype)


def _modulate(x, shift, scale, tm=256):
    t, d = x.shape
    tm = min(tm, t)
    vec = pl.BlockSpec((1, d), lambda i: (0, 0))
    return pl.pallas_call(
        _modulate_kernel,
        grid=(t // tm,),
        in_specs=[pl.BlockSpec((tm, d), lambda i: (i, 0)), vec, vec],
        out_specs=pl.BlockSpec((tm, d), lambda i: (i, 0)),
        out_shape=jax.ShapeDtypeStruct((t, d), BF16),
        compiler_params=_params(("parallel",)),
        name="modulate",
    )(x, shift, scale)


def _cast_kernel(x_ref, o_ref):
    o_ref[...] = x_ref[...].astype(o_ref.dtype)


def _cast_bf16(x, tm=512):
    r, c = x.shape
    return pl.pallas_call(
        _cast_kernel,
        grid=(r // tm,),
        in_specs=[pl.BlockSpec((tm, c), lambda i: (i, 0))],
        out_specs=pl.BlockSpec((tm, c), lambda i: (i, 0)),
        out_shape=jax.ShapeDtypeStruct((r, c), BF16),
        compiler_params=_params(("parallel",)),
        name="cast_bf16",
    )(x)


def _matmul_kernel(*refs, n_a, act):
    a_refs, w_ref, o_ref = refs[:n_a], refs[n_a], refs[n_a + 1]
    w = w_ref[...].astype(BF16)
    acc = None
    k0 = 0
    for a_ref in a_refs:
        k1 = k0 + a_ref.shape[1]
        part = jnp.dot(a_ref[...], w[k0:k1], preferred_element_type=F32)
        acc = part if acc is None else acc + part
        k0 = k1
    if act == "gelu":
        acc = jax.nn.gelu(acc)
    o_ref[...] = acc.astype(o_ref.dtype)


def _matmul(a_list, w, out_dtype, *, n_out=None, col_off=0, act=None, bm=1024, bn=512, name="matmul"):
    m = a_list[0].shape[0]
    k = w.shape[0]
    n_out = w.shape[1] if n_out is None else n_out
    bm = min(bm, m)
    assert sum(a.shape[1] for a in a_list) == k
    assert m % bm == 0 and n_out % bn == 0 and col_off % bn == 0
    off_blocks = col_off // bn
    in_specs = [pl.BlockSpec((bm, a.shape[1]), lambda i, j: (i, 0)) for a in a_list]
    in_specs.append(pl.BlockSpec((k, bn), lambda i, j: (0, j + off_blocks)))
    return pl.pallas_call(
        functools.partial(_matmul_kernel, n_a=len(a_list), act=act),
        grid=(m // bm, n_out // bn),
        in_specs=in_specs,
        out_specs=pl.BlockSpec((bm, bn), lambda i, j: (i, j)),
        out_shape=jax.ShapeDtypeStruct((m, n_out), out_dtype),
        compiler_params=_params(("parallel", "parallel")),
        name=name,
    )(*a_list, w)


def _resln_kernel(x_ref, y_ref, gate_ref, g_ref, b_ref, shift_ref, scale_ref, xo_ref, ho_ref, *hto_ref):
    r = DEEPNORM_ALPHA * x_ref[...] + gate_ref[...] * y_ref[...]
    xn = _ln_rows(r, g_ref[...], b_ref[...])
    xo_ref[...] = xn
    h = xn * (1.0 + scale_ref[...]) + shift_ref[...]
    ho_ref[...] = h.astype(ho_ref.dtype)
    if hto_ref:
        hto_ref[0][...] = h.T.astype(hto_ref[0].dtype)


def _resln(x, y, gate, g, b, shift, scale, transposed=False, tm=256):
    t, d = x.shape
    tm = min(tm, t)
    row = pl.BlockSpec((tm, d), lambda i: (i, 0))
    vec = pl.BlockSpec((1, d), lambda i: (0, 0))
    out_specs = [row, row]
    out_shape = [jax.ShapeDtypeStruct((t, d), F32), jax.ShapeDtypeStruct((t, d), BF16)]
    if transposed:
        out_specs.append(pl.BlockSpec((d, tm), lambda i: (0, i)))
        out_shape.append(jax.ShapeDtypeStruct((d, t), BF16))
    return pl.pallas_call(
        _resln_kernel,
        grid=(t // tm,),
        in_specs=[row, row, vec, vec, vec, vec, vec],
        out_specs=out_specs,
        out_shape=out_shape,
        compiler_params=_params(("parallel",)),
        name="resln",
    )(x, y, gate, g.reshape(1, d), b.reshape(1, d), shift, scale)


def _qk_prep_kernel(z_ref, gain_ref, cos_ref, sin_ref, o_ref, *, n_q_blocks, q_scale):
    quarter = HEAD_DIM // 4
    cos_t, sin_t = cos_ref[...], sin_ref[...]
    lane = lax.broadcasted_iota(jnp.int32, cos_t.shape, 1)
    low_half = (lane & quarter) == 0
    for hb in range(gain_ref.shape[0]):
        cols = slice(hb * HEAD_DIM, (hb + 1) * HEAD_DIM)
        x = z_ref[:, cols]
        x = x * lax.rsqrt(jnp.mean(x * x, axis=-1, keepdims=True) + LN_EPS) * gain_ref[hb]
        partner = jnp.where(low_half,
                            pltpu.roll(x, HEAD_DIM - quarter, axis=1),
                            pltpu.roll(x, quarter, axis=1))
        y = x * cos_t + partner * sin_t
        if hb < n_q_blocks:
            y = y * q_scale
        o_ref[:, cols] = y.astype(o_ref.dtype)


def _qk_prep(z, gains, cos_t, sin_t, n_blocks, n_q_blocks, q_scale, tm=256):
    t = z.shape[0]
    tm = min(tm, t)
    width = n_blocks * HEAD_DIM
    return pl.pallas_call(
        functools.partial(_qk_prep_kernel, n_q_blocks=n_q_blocks, q_scale=q_scale),
        grid=(t // tm,),
        in_specs=[pl.BlockSpec((tm, width), lambda i: (i, 0)),
                  pl.BlockSpec((n_blocks, 1, HEAD_DIM), lambda i: (0, 0, 0)),
                  pl.BlockSpec((tm, HEAD_DIM), lambda i: (i, 0)),
                  pl.BlockSpec((tm, HEAD_DIM), lambda i: (i, 0))],
        out_specs=pl.BlockSpec((tm, width), lambda i: (i, 0)),
        out_shape=jax.ShapeDtypeStruct((t, width), BF16),
        compiler_params=_params(("parallel",)),
        name="qk_prep",
    )(z, gains, cos_t, sin_t)


def _flash_kernel(q_ref, k_ref, v_ref, o_ref, m_ref, l_ref, acc_ref):
    kt = pl.program_id(2)

    @pl.when(kt == 0)
    def _():
        m_ref[...] = jnp.full(m_ref.shape, -jnp.inf, F32)
        l_ref[...] = jnp.zeros(l_ref.shape, F32)
        acc_ref[...] = jnp.zeros(acc_ref.shape, F32)

    k = k_ref[...]
    v = v_ref[...]
    for g in range(GQA_GROUP):
        cols = slice(g * HEAD_DIM, (g + 1) * HEAD_DIM)
        s_t = lax.dot_general(k, q_ref[:, cols], _NT, preferred_element_type=F32)
        m_prev = m_ref[g]
        m_new = jnp.maximum(m_prev, jnp.max(s_t, axis=0, keepdims=True))
        alpha = jnp.exp2(m_prev - m_new)
        p = jnp.exp2(s_t - m_new)
        l_ref[g] = alpha * l_ref[g] + jnp.sum(p, axis=0, keepdims=True)
        acc_ref[g] = alpha * acc_ref[g] + lax.dot_general(
            v, p.astype(v.dtype), _TN, preferred_element_type=F32)
        m_ref[g] = m_new

    @pl.when(kt == pl.num_programs(2) - 1)
    def _():
        for g in range(GQA_GROUP):
            cols = slice(g * HEAD_DIM, (g + 1) * HEAD_DIM)
            o_ref[:, cols] = (acc_ref[g] / l_ref[g]).T.astype(o_ref.dtype)


def _flash(q, k_all, v_all, tq=1024, tk=768):
    s_q = q.shape[0]
    s_k = k_all.shape[0]
    tq = min(tq, s_q)
    if s_k % tk:
        tk = 256
    gw = GQA_GROUP * HEAD_DIM
    return pl.pallas_call(
        _flash_kernel,
        grid=(KV_HEADS, s_q // tq, s_k // tk),
        in_specs=[pl.BlockSpec((tq, gw), lambda h, i, j: (i, h)),
                  pl.BlockSpec((tk, HEAD_DIM), lambda h, i, j: (j, h)),
                  pl.BlockSpec((tk, HEAD_DIM), lambda h, i, j: (j, h))],
        out_specs=pl.BlockSpec((tq, gw), lambda h, i, j: (i, h)),
        out_shape=jax.ShapeDtypeStruct((s_q, KV_HEADS * gw), BF16),
        scratch_shapes=[pltpu.VMEM((GQA_GROUP, 1, tq), F32),
                        pltpu.VMEM((GQA_GROUP, 1, tq), F32),
                        pltpu.VMEM((GQA_GROUP, HEAD_DIM, tq), F32)],
        compiler_params=_params(("parallel", "parallel", "arbitrary")),
        name="flash",
    )(q, k_all, v_all)


_CONV_HALO = 16
_CONV_ROWS = 64


def _conv_kernel(*refs, n_cb, cb_w, n_taps):
    val_refs = refs[0:3 * n_cb:3]
    val_prev = refs[1:3 * n_cb:3]
    val_next = refs[2:3 * n_cb:3]
    gate_refs = refs[3 * n_cb:6 * n_cb:3]
    gate_prev = refs[3 * n_cb + 1:6 * n_cb:3]
    gate_next = refs[3 * n_cb + 2:6 * n_cb:3]
    dw_ref, dwb_ref, g_ref, b_ref, o_ref, hbuf_ref, shift_ref, conv_ref = refs[6 * n_cb:]
    i = pl.program_id(0)
    tm = o_ref.shape[0]
    has_prev = (i > 0).astype(F32)
    has_next = (i < pl.num_programs(0) - 1).astype(F32)
    pad = n_taps // 2
    base = _CONV_HALO - pad
    n_shift_rows = shift_ref.shape[1]

    def glu(vr, gr):
        return vr[...] * jax.nn.sigmoid(gr[...])

    for cb in range(n_cb):
        hbuf_ref[0:_CONV_HALO, :] = glu(val_prev[cb], gate_prev[cb]) * has_prev
        hbuf_ref[_CONV_HALO:_CONV_HALO + tm, :] = glu(val_refs[cb], gate_refs[cb])
        hbuf_ref[_CONV_HALO + tm:, :] = glu(val_next[cb], gate_next[cb]) * has_next
        for sh in range(SUBLANES):
            shift_ref[sh] = hbuf_ref[sh:sh + n_shift_rows, :]

        def row_chunk(rc, carry, cb=cb):
            r0 = pl.multiple_of(rc * _CONV_ROWS, _CONV_ROWS)
            for lc in range(cb_w // LANES):
                lanes = slice(lc * LANES, (lc + 1) * LANES)
                wl = slice(cb * cb_w + lc * LANES, cb * cb_w + (lc + 1) * LANES)
                acc = jnp.zeros((_CONV_ROWS, LANES), F32) + dwb_ref[:, wl]
                for kk in range(n_taps):
                    sh, blk = (base + kk) % SUBLANES, (base + kk) // SUBLANES
                    win = shift_ref[sh, pl.ds(r0 + blk * SUBLANES, _CONV_ROWS), lanes]
                    acc = acc + win * dw_ref[kk:kk + 1, wl]
                conv_ref[pl.ds(r0, _CONV_ROWS), wl] = acc
            return carry

        lax.fori_loop(0, tm // _CONV_ROWS, row_chunk, 0)

    y = _ln_rows(conv_ref[...], g_ref[...], b_ref[...])
    o_ref[...] = (y * jax.nn.sigmoid(y)).astype(o_ref.dtype)


def _conv_branch(z, col0, dw_w, dw_b, ln_g, ln_b, tm=256, cb_w=1024):
    t = z.shape[0]
    n_taps, width = dw_w.shape
    tm = min(tm, t)
    n_cb = width // cb_w
    assert col0 % cb_w == 0 and n_taps // 2 < _CONV_HALO and tm % _CONV_ROWS == 0
    hb = tm // _CONV_HALO
    last_hb = t // _CONV_HALO - 1

    def specs(first_block):
        out = []
        for cb in range(n_cb):
            c = first_block + cb
            out.append(pl.BlockSpec((tm, cb_w), lambda i, c=c: (i, c)))
            out.append(pl.BlockSpec((_CONV_HALO, cb_w), lambda i, c=c: (jnp.maximum(i * hb - 1, 0), c)))
            out.append(pl.BlockSpec((_CONV_HALO, cb_w), lambda i, c=c: (jnp.minimum((i + 1) * hb, last_hb), c)))
        return out

    vec = pl.BlockSpec((1, width), lambda i: (0, 0))
    in_specs = specs(col0 // cb_w) + specs((col0 + width) // cb_w)
    in_specs += [pl.BlockSpec((n_taps, width), lambda i: (0, 0)), vec, vec, vec]
    return pl.pallas_call(
        functools.partial(_conv_kernel, n_cb=n_cb, cb_w=cb_w, n_taps=n_taps),
        grid=(t // tm,),
        in_specs=in_specs,
        out_specs=pl.BlockSpec((tm, width), lambda i: (i, 0)),
        out_shape=jax.ShapeDtypeStruct((t, width), BF16),
        scratch_shapes=[pltpu.VMEM((tm + 2 * _CONV_HALO, cb_w), F32),
                        pltpu.VMEM((SUBLANES, tm + 2 * _CONV_HALO - SUBLANES, cb_w), F32),
                        pltpu.VMEM((tm, width), F32)],
        compiler_params=_params(("parallel",)),
        name="conv_branch",
    )(*([z] * (6 * n_cb)), dw_w, dw_b.reshape(1, width), ln_g.reshape(1, width), ln_b.reshape(1, width))


def _sgu_kernel(u_ref, v_ref, g_ref, b_ref, w_ref, bias_ref, o_ref):
    n_heads, chunk, _ = w_ref.shape
    hw = u_ref.shape[1] // n_heads
    for c in range(u_ref.shape[0] // chunk):
        rows = slice(c * chunk, (c + 1) * chunk)
        vn = _ln_rows(v_ref[rows, :], g_ref[...], b_ref[...]).astype(BF16)
        for hd in range(n_heads):
            cols = slice(hd * hw, (hd + 1) * hw)
            mixed = jnp.dot(w_ref[hd].astype(BF16), vn[:, cols], preferred_element_type=F32) + bias_ref[hd]
            o_ref[rows, cols] = (u_ref[rows, cols] * mixed).astype(o_ref.dtype)


def _sgu(g_act, ln_g, ln_b, sgu_w, sgu_b, tm=256):
    t, two_w = g_act.shape
    width = two_w // 2
    n_heads, chunk, _ = sgu_w.shape
    tm = min(tm, t)
    hw = width // n_heads
    bias = jnp.broadcast_to(sgu_b[:, :, None], (n_heads, chunk, hw))
    vec = pl.BlockSpec((1, width), lambda i: (0, 0))
    return pl.pallas_call(
        _sgu_kernel,
        grid=(t // tm,),
        in_specs=[pl.BlockSpec((tm, width), lambda i: (i, 0)),
                  pl.BlockSpec((tm, width), lambda i: (i, 1)),
                  vec, vec,
                  pl.BlockSpec((n_heads, chunk, chunk), lambda i: (0, 0, 0)),
                  pl.BlockSpec((n_heads, chunk, hw), lambda i: (0, 0, 0))],
        out_specs=pl.BlockSpec((tm, width), lambda i: (i, 0)),
        out_shape=jax.ShapeDtypeStruct((t, width), BF16),
        compiler_params=_params(("parallel",)),
        name="sgu",
    )(g_act, g_act, ln_g.reshape(1, width), ln_b.reshape(1, width), sgu_w, bias)


def _top_ranked(s, key, n, big_key, want_rank):
    vals, firsts = [], []
    rank = jnp.full(s.shape, float(n), F32) if want_rank else None
    for r in range(n):
        m = jnp.max(s, axis=0, keepdims=True)
        first = jnp.min(jnp.where(s == m, key, big_key), axis=0, keepdims=True)
        hit = key == first
        s = jnp.where(hit, -jnp.inf, s)
        if want_rank:
            rank = jnp.where(hit, float(r), rank)
        vals.append(m)
        firsts.append(first)
    return vals, firsts, rank


def _route_kernel(q_ref, k1_ref, k2_ref, a_ref, len_ref, r2_ref, e2_ref, v1_ref, v2_ref):
    n_heads, n_keys, kd = k1_ref.shape
    tm = q_ref.shape[0]
    topk = PEER_TOPK
    half = topk // 2
    key_iota = lax.broadcasted_iota(jnp.int32, (n_keys, tm), 0).astype(F32)
    sub = lax.broadcasted_iota(jnp.int32, (SUBLANES, tm), 0).astype(F32)
    for hd in range(n_heads):
        q1 = q_ref[:, (2 * hd) * kd:(2 * hd + 1) * kd]
        q2 = q_ref[:, (2 * hd + 1) * kd:(2 * hd + 2) * kd]
        s1 = lax.dot_general(k1_ref[hd].astype(BF16), q1, _NT, preferred_element_type=F32)
        s2 = lax.dot_general(k2_ref[hd].astype(BF16), q2, _NT, preferred_element_type=F32)
        v1, _, rank1 = _top_ranked(s1, key_iota, topk, float(n_keys), True)
        v2, _, rank2 = _top_ranked(s2, key_iota, topk, float(n_keys), True)
        for r in range(topk):
            v1_ref[r:r + 1, :] = v1[r]
            v2_ref[r:r + 1, :] = v2[r]
        v2_lo, v2_hi = v2_ref[0:half, :], v2_ref[half:topk, :]
        cands = [v1[0] + v2_lo, v1[0] + v2_hi]
        keys = [sub, sub + half]
        for a in range(1, half):
            cands.append(v1[a] + v2_lo)
            keys.append(sub + a * topk)
        cands.append(v1_ref[half:topk, :] + v2[0])
        keys.append((sub + half) * topk)
        cand = jnp.concatenate(cands, axis=0)
        ckey = jnp.concatenate(keys, axis=0)
        top, top_keys, _ = _top_ranked(cand, ckey, topk, float(topk * topk), False)
        z = jnp.zeros_like(top[0])
        n_sel = jnp.zeros((n_keys, tm), F32)
        for r in range(topk):
            z = z + jnp.exp(top[r] - top[0])
            rank_a = jnp.floor(top_keys[r] * (1.0 / topk))
            n_sel = n_sel + jnp.where(rank1 == rank_a, 1.0, 0.0)
        a_ref[hd] = jnp.exp(s1 - v1[0]) / z
        len_ref[hd] = n_sel
        r2_ref[hd] = rank2.astype(r2_ref.dtype)
        e2_ref[hd] = jnp.exp(s2 - v2[0]).astype(e2_ref.dtype)


def _route(q, k1, k2, tm=256):
    t = q.shape[0]
    n_heads, n_keys, kd = k1.shape
    tm = min(tm, t)
    kspec = pl.BlockSpec((n_heads, n_keys, kd), lambda i: (0, 0, 0))
    ospec = pl.BlockSpec((n_heads, n_keys, tm), lambda i: (0, 0, i))
    oshape = [jax.ShapeDtypeStruct((n_heads, n_keys, t), dt) for dt in (F32, F32, BF16, BF16)]
    return pl.pallas_call(
        _route_kernel,
        grid=(t // tm,),
        in_specs=[pl.BlockSpec((tm, 2 * n_heads * kd), lambda i: (i, 0)), kspec, kspec],
        out_specs=[ospec] * 4,
        out_shape=oshape,
        scratch_shapes=[pltpu.VMEM((PEER_TOPK, tm), F32), pltpu.VMEM((PEER_TOPK, tm), F32)],
        compiler_params=_params(("parallel",)),
        name="peer_route",
    )(q, k1, k2)


def _gelu_tanh(x):
    c1 = -2.0 * 0.7978845608028654 * LOG2_E
    e = jnp.exp2(x * (c1 + (c1 * 0.044715) * (x * x)))
    return x / (1.0 + e)


def _peer_kernel(ht_ref, u_ref, v_ref, a_ref, len_ref, r2_ref, e2_ref, o_ref, act0_ref, act1_ref, *, n_tiles):
    j = pl.program_id(1)
    n_heads, n_keys, _ = r2_ref.shape
    n_i = u_ref.shape[0] // n_keys
    act_refs = (act0_ref, act1_ref)

    def first_matmul(slot):
        act_refs[slot][...] = jnp.dot(u_ref[...], ht_ref[...], preferred_element_type=F32)

    def second_matmul(slot):
        act = _gelu_tanh(act_refs[slot][...]).astype(BF16)
        zero = jnp.zeros((), BF16)
        blocks = []
        for ii in range(n_i):
            i = (j - 1) * n_i + ii
            gate = None
            for hd in range(n_heads):
                a_row = a_ref[hd, pl.ds(i, 1), :].astype(BF16)
                len_row = len_ref[hd, pl.ds(i, 1), :].astype(BF16)
                term = jnp.where(r2_ref[hd] < len_row, a_row * e2_ref[hd], zero)
                gate = term if gate is None else gate + term
            blocks.append(gate * act[ii * n_keys:(ii + 1) * n_keys, :])
        w_t = jnp.concatenate(blocks, axis=0)
        o_ref[...] += lax.dot_general(w_t, v_ref[...], _TN, preferred_element_type=F32)

    @pl.when(j == 0)
    def _():
        o_ref[...] = jnp.zeros(o_ref.shape, F32)
        first_matmul(0)

    for parity in (0, 1):
        @pl.when((j > 0) & (j < n_tiles) & (j % 2 == parity))
        def _(parity=parity):
            first_matmul(parity)
            second_matmul(1 - parity)

    @pl.when(j == n_tiles)
    def _():
        second_matmul((n_tiles - 1) % 2)


def _peer_dense(h_t, u_bf, v_bf, route, tm=512, te=512):
    d, t = h_t.shape
    n_exp = u_bf.shape[0]
    n_heads, n_keys, _ = route[0].shape
    tm = min(tm, t)
    assert te % n_keys == 0 and n_exp % te == 0
    n_tiles = n_exp // te
    once = dict(pipeline_mode=pl.Buffered(1))
    rspec = pl.BlockSpec((n_heads, n_keys, tm), lambda i, j: (0, 0, i), **once)
    return pl.pallas_call(
        functools.partial(_peer_kernel, n_tiles=n_tiles),
        grid=(t // tm, n_tiles + 1),
        in_specs=[pl.BlockSpec((d, tm), lambda i, j: (0, i), **once),
                  pl.BlockSpec((te, d), lambda i, j: (jnp.minimum(j, n_tiles - 1), 0)),
                  pl.BlockSpec((te, d), lambda i, j: (jnp.maximum(j - 1, 0), 0)),
                  rspec, rspec, rspec, rspec],
        out_specs=pl.BlockSpec((tm, d), lambda i, j: (i, 0)),
        out_shape=jax.ShapeDtypeStruct((t, d), F32),
        scratch_shapes=[pltpu.VMEM((te, tm), F32), pltpu.VMEM((te, tm), F32)],
        compiler_params=_params(("parallel", "arbitrary")),
        name="peer_dense",
    )(h_t, u_bf, v_bf, *route)


def _peer_ffn(h, h_t, w_q, k1, k2, u_tab, v_tab):
    q = _matmul([h], w_q, BF16, name="peer_wq")
    route = _route(q, k1, k2)
    return _peer_dense(h_t, _cast_bf16(u_tab), _cast_bf16(v_tab), route)


def _rope_tables(seq):
    n_freq = HEAD_DIM // 4
    inv = ROPE_THETA ** (-jnp.arange(n_freq, dtype=F32) / n_freq)
    pos = jnp.arange(seq, dtype=jnp.int32)
    ang_r = (pos // GRID_W).astype(F32)[:, None] * inv
    ang_c = (pos % GRID_W).astype(F32)[:, None] * inv
    cos_t = jnp.concatenate([jnp.cos(ang_r)] * 2 + [jnp.cos(ang_c)] * 2, axis=-1)
    sin_t = jnp.concatenate([-jnp.sin(ang_r), jnp.sin(ang_r), -jnp.sin(ang_c), jnp.sin(ang_c)], axis=-1)
    return cos_t, sin_t


def _attn_conv_mixer(h, hc, w_in, q_gain, k_gain, dw_w, dw_b, conv_ln_g, conv_ln_b, w_out):
    seq = h.shape[0]
    n_ctx = hc.shape[0]
    attn_w = KV_HEADS * GQA_GROUP * HEAD_DIM
    kv_w = KV_HEADS * HEAD_DIM
    n_q, n_k = attn_w // HEAD_DIM, kv_w // HEAD_DIM
    z = _matmul([h], w_in, F32, name="attn_w_in")
    kvc = _matmul([hc], w_in, F32, n_out=2 * kv_w, col_off=attn_w, name="ctx_kv")
    cos_t, sin_t = _rope_tables(seq)
    gains = jnp.concatenate([jnp.broadcast_to(q_gain, (n_q, HEAD_DIM)),
                             jnp.broadcast_to(k_gain, (n_k, HEAD_DIM))]).reshape(n_q + n_k, 1, HEAD_DIM)
    qk = _qk_prep(z, gains, cos_t, sin_t, n_q + n_k, n_q, HEAD_DIM ** -0.5 * LOG2_E)
    kc = _qk_prep(kvc, gains[n_q:], jnp.ones((n_ctx, HEAD_DIM), F32), jnp.zeros((n_ctx, HEAD_DIM), F32),
                  n_k, 0, 1.0)
    k_all = jnp.concatenate([kc, qk[:, attn_w:]], axis=0)
    v_all = jnp.concatenate([kvc[:, kv_w:], z[:, attn_w + kv_w:attn_w + 2 * kv_w]], axis=0).astype(BF16)
    o_attn = _flash(qk, k_all, v_all)
    o_conv = _conv_branch(z, attn_w + 2 * kv_w, dw_w, dw_b, conv_ln_g, conv_ln_b)
    return _matmul([o_attn, o_conv], w_out, F32, name="attn_w_out")


def _gmlp_mixer(h, w_in, sgu_ln_g, sgu_ln_b, sgu_w, sgu_b, w_out):
    g_act = _matmul([h], w_in, F32, act="gelu", name="gmlp_w_in")
    gated = _sgu(g_act, sgu_ln_g, sgu_ln_b, sgu_w, sgu_b)
    return _matmul([gated], w_out, F32, name="gmlp_w_out")


def kernel(x, c, ctx, c_ctx, l0_w_mod, l0_b_mod, l0_w_in, l0_q_gain, l0_k_gain, l0_dw_w, l0_dw_b, l0_conv_ln_g, l0_conv_ln_b, l0_w_out, l0_mix_ln_g, l0_mix_ln_b, l0_peer_wq, l0_peer_k1, l0_peer_k2, l0_peer_u, l0_peer_v, l0_ffn_ln_g, l0_ffn_ln_b, l1_w_mod, l1_b_mod, l1_w_in, l1_sgu_ln_g, l1_sgu_ln_b, l1_sgu_w, l1_sgu_b, l1_w_out, l1_mix_ln_g, l1_mix_ln_b, l1_peer_wq, l1_peer_k1, l1_peer_k2, l1_peer_u, l1_peer_v, l1_ffn_ln_g, l1_ffn_ln_b):
    b, s, d = x.shape
    assert b == 1 and c.shape[0] == 1 and ctx.shape[0] == 1
    xs = x.reshape(s, d)
    ctx2 = ctx.reshape(ctx.shape[1], d)
    c_cols = jnp.stack([c.reshape(d), c_ctx], axis=1)

    def mods(w_mod, b_mod):
        mod = _modvec(c_cols, w_mod, b_mod)
        return [mod[:, k * d:(k + 1) * d] for k in range(N_MOD)]

    m0 = mods(l0_w_mod, l0_b_mod)
    m1 = mods(l1_w_mod, l1_b_mod)
    shift_m, scale_m, gate_m, shift_f, scale_f, gate_f = [v[0:1] for v in m0]
    h = _modulate(xs, shift_m, scale_m)
    hc = _modulate(ctx2, m0[0][1:2], m0[1][1:2])
    y = _attn_conv_mixer(h, hc, l0_w_in, l0_q_gain, l0_k_gain, l0_dw_w, l0_dw_b,
                         l0_conv_ln_g, l0_conv_ln_b, l0_w_out)
    xs, h, h_t = _resln(xs, y, gate_m, l0_mix_ln_g, l0_mix_ln_b, shift_f, scale_f, transposed=True)
    y = _peer_ffn(h, h_t, l0_peer_wq, l0_peer_k1, l0_peer_k2, l0_peer_u, l0_peer_v)
    shift_m, scale_m, gate_m1, shift_f1, scale_f1, gate_f1 = [v[0:1] for v in m1]
    xs, h = _resln(xs, y, gate_f, l0_ffn_ln_g, l0_ffn_ln_b, shift_m, scale_m)

    y = _gmlp_mixer(h, l1_w_in, l1_sgu_ln_g, l1_sgu_ln_b, l1_sgu_w, l1_sgu_b, l1_w_out)
    xs, h, h_t = _resln(xs, y, gate_m1, l1_mix_ln_g, l1_mix_ln_b, shift_f1, scale_f1, transposed=True)
    y = _peer_ffn(h, h_t, l1_peer_wq, l1_peer_k1, l1_peer_k2, l1_peer_u, l1_peer_v)
    zero = jnp.zeros((1, d), F32)
    xs, _ = _resln(xs, y, gate_f1, l1_ffn_ln_g, l1_ffn_ln_b, zero, zero)
    return xs.reshape(b, s, d)
```
